```python
import jax
import jax.numpy as jnp
from jax import lax
import numpy as np

D_MODEL = 1024
BATCH = 2
SEQ = 8192
DEPTH = 4
DEC_BATCH = 128
DEC_SEQ = 4
PAST_LEN = 2048
PAGE_SIZE = 128

HEAD_DIM = 64
N_HEADS = D_MODEL // HEAD_DIM
NSA_KV_HEADS = 4
NSA_GROUP = N_HEADS // NSA_KV_HEADS
NSA_BLOCK = 64
NSA_TOP_N = 16
NSA_WINDOW = 512
Q_BLOCK = 128
ROPE_THETA = 10000.0
D_FF = -(-(8 * D_MODEL) // (3 * 256)) * 256
N_MIXERS = 2
N_FOX = (DEPTH + 1) // 2
N_NSA = DEPTH // 2
DN_ALPHA = (2 * DEPTH) ** 0.25
DN_BETA = (8 * DEPTH) ** -0.25
LN_EPS = 1e-5
NEG_BIG = -1e30
SEL_FORCE = 1e9
ATTN_WIDTH = N_HEADS * HEAD_DIM
KV_WIDTH = NSA_KV_HEADS * HEAD_DIM
FOX_IN = 3 * ATTN_WIDTH + N_HEADS
NSA_IN = ATTN_WIDTH + 6 * KV_WIDTH + 3 * N_HEADS

kernel_name = "fox_nsa_deepnorm_decode_step"


def layer_norm(x, g, b):
    xf = x.astype(jnp.float32)
    mu = jnp.mean(xf, axis=-1, keepdims=True)
    var = jnp.mean(jnp.square(xf - mu), axis=-1, keepdims=True)
    return ((xf - mu) * lax.rsqrt(var + LN_EPS) * g + b).astype(x.dtype)


def post_norm(x, h, g, b):
    return layer_norm(DN_ALPHA * x + h, g, b)


def swiglu(x, w_gu, w_down):
    gate, up = jnp.split(x @ w_gu, 2, axis=-1)
    return (jax.nn.silu(gate) * up) @ w_down


def rope(x, pos):
    half = HEAD_DIM // 2
    inv_freq = ROPE_THETA ** (-jnp.arange(half, dtype=jnp.float32) / half)
    ang = pos.astype(jnp.float32)[:, None] * inv_freq[None, :]
    cos = jnp.cos(ang)[None, :, None, :]
    sin = jnp.sin(ang)[None, :, None, :]
    x1 = x[..., :half].astype(jnp.float32)
    x2 = x[..., half:].astype(jnp.float32)
    return jnp.concatenate([x1 * cos - x2 * sin, x2 * cos + x1 * sin], axis=-1).astype(x.dtype)


def masked_softmax(s, mask):
    s = jnp.where(mask, s.astype(jnp.float32), NEG_BIG)
    p = jnp.where(mask, jnp.exp(s - jnp.max(s, axis=-1, keepdims=True)), 0.0)
    return p / jnp.maximum(jnp.sum(p, axis=-1, keepdims=True), 1e-30)


def pad_rows(x, n):
    return jnp.pad(x, [(0, 0), (0, n - x.shape[1])] + [(0, 0)] * (x.ndim - 2))


def gather_pages(pool, layer, page_table):
    rows = pool[layer, page_table]
    return rows.reshape((page_table.shape[0], -1) + pool.shape[3:])


def fox_project(x, w_in, b_f):
    B, T, _ = x.shape
    z = x @ w_in
    q, k, v = (z[..., i * ATTN_WIDTH:(i + 1) * ATTN_WIDTH].reshape(B, T, N_HEADS, HEAD_DIM) for i in range(3))
    logf = jax.nn.log_sigmoid((z[..., 3 * ATTN_WIDTH:] + b_f).astype(jnp.float32))
    return q, k, v, logf


def fox_attend(q, k, v, f_q, f_k, q_pos, k_pos):
    B, Tq = q.shape[:2]
    s = jnp.einsum("bqhd,bkhd->bhqk", q, k).astype(jnp.float32) * HEAD_DIM ** -0.5
    s = s + jnp.swapaxes(f_q, 1, 2)[:, :, :, None] - jnp.swapaxes(f_k, 1, 2)[:, :, None, :]
    p = masked_softmax(s, k_pos[None, :] <= q_pos[:, None])
    return jnp.einsum("bhqk,bkhd->bqhd", p.astype(v.dtype), v).reshape(B, Tq, ATTN_WIDTH)


def fox_prompt(x, w_in, b_f, w_o):
    B, T, _ = x.shape
    q, k, v, logf = fox_project(x, w_in, b_f)
    f_cum = jnp.cumsum(logf, axis=1)
    pos = jnp.arange(T)

    def block(i):
        q0 = i * Q_BLOCK
        return fox_attend(lax.dynamic_slice_in_dim(q, q0, Q_BLOCK, 1), k, v,
                          lax.dynamic_slice_in_dim(f_cum, q0, Q_BLOCK, 1), f_cum,
                          q0 + jnp.arange(Q_BLOCK), pos)

    o = lax.map(block, jnp.arange(T // Q_BLOCK))
    o = jnp.swapaxes(o, 0, 1).reshape(B, T, ATTN_WIDTH)
    return o @ w_o, (k, v, logf)


def fox_sample(x, layer, cache_k, cache_v, cache_logf, page_table, w_in, b_f, w_o):
    _, T, _ = x.shape
    past = page_table.shape[1] * PAGE_SIZE
    q, k, v, logf = fox_project(x, w_in, b_f)
    k_all = jnp.concatenate([gather_pages(cache_k, layer, page_table), k], axis=1)
    v_all = jnp.concatenate([gather_pages(cache_v, layer, page_table), v], axis=1)
    lf_all = jnp.concatenate([gather_pages(cache_logf, layer, page_table).astype(jnp.float32), logf], axis=1)
    f_cum = jnp.cumsum(lf_all, axis=1)
    q_pos = past + jnp.arange(T)
    o = fox_attend(q, k_all, v_all, f_cum[:, past:], f_cum, q_pos, jnp.arange(past + T))
    return o @ w_o, (k, v, logf)


def nsa_project(x, w_in, b_gate, pos):
    B, T, _ = x.shape
    z = x @ w_in
    q = z[..., :ATTN_WIDTH].reshape(B, T, N_HEADS, HEAD_DIM)
    kv = z[..., ATTN_WIDTH:ATTN_WIDTH + 6 * KV_WIDTH].reshape(B, T, 6, NSA_KV_HEADS, HEAD_DIM)
    gates = jax.nn.sigmoid(z[..., ATTN_WIDTH + 6 * KV_WIDTH:] + b_gate).reshape(B, T, 3, N_HEADS)
    return (q, rope(q, pos), kv[:, :, 0], kv[:, :, 1], rope(kv[:, :, 2], pos), kv[:, :, 3],
            rope(kv[:, :, 4], pos), kv[:, :, 5], gates)


def block_summaries(x, w_cmp):
    B, L = x.shape[:2]
    nb = -(-L // NSA_BLOCK)
    xb = pad_rows(x, nb * NSA_BLOCK).reshape(B, nb, NSA_BLOCK, NSA_KV_HEADS, HEAD_DIM)
    return jnp.einsum("bnjgd,jg->bngd", xb, w_cmp)


def nsa_block(q_raw, q_rot, gates, q_pos, kc, vc, ks, vs, kw, vw, kw_pos):
    B, Tq = q_raw.shape[:2]
    G, R = NSA_KV_HEADS, NSA_GROUP
    nb = kc.shape[1]
    scale = HEAD_DIM ** -0.5
    qc = q_raw.reshape(B, Tq, G, R, HEAD_DIM)
    qr = q_rot.reshape(B, Tq, G, R, HEAD_DIM)
    blk = jnp.arange(nb)
    s_c = jnp.einsum("btgrd,bngd->bgrtn", qc, kc) * scale
    p_c = masked_softmax(s_c, (blk[None, :] + 1) * NSA_BLOCK - 1 <= q_pos[:, None])
    o_c = jnp.einsum("bgrtn,bngd->btgrd", p_c.astype(vc.dtype), vc)
    cur = q_pos[:, None] // NSA_BLOCK
    forced = (blk[None, :] == 0) | (blk[None, :] == cur) | (blk[None, :] == cur - 1)
    score = jnp.where(blk[None, :] <= cur, jnp.where(forced, SEL_FORCE, jnp.sum(p_c, axis=2)), -SEL_FORCE)
    n_sel = min(NSA_TOP_N, nb)
    _, idx = lax.top_k(score, n_sel)
    take = jax.vmap(jax.vmap(lambda blocks, i: blocks[i]))
    kb = jnp.transpose(ks.reshape(B, nb, NSA_BLOCK, G, HEAD_DIM), (0, 3, 1, 2, 4))
    vb = jnp.transpose(vs.reshape(B, nb, NSA_BLOCK, G, HEAD_DIM), (0, 3, 1, 2, 4))
    k_sel = take(kb, idx).reshape(B, G, Tq, n_sel * NSA_BLOCK, HEAD_DIM)
    v_sel = take(vb, idx).reshape(B, G, Tq, n_sel * NSA_BLOCK, HEAD_DIM)
    sel_pos = (idx[..., None] * NSA_BLOCK + jnp.arange(NSA_BLOCK)).reshape(B, G, 1, Tq, n_sel * NSA_BLOCK)
    s_s = jnp.einsum("btgrd,bgtkd->bgrtk", qr, k_sel) * scale
    p_s = masked_softmax(s_s, sel_pos <= q_pos[:, None])
    o_s = jnp.einsum("bgrtk,bgtkd->btgrd", p_s.astype(vs.dtype), v_sel)
    s_w = jnp.einsum("btgrd,blgd->bgrtl", qr, kw) * scale
    dist = q_pos[:, None] - kw_pos[None, :]
    p_w = masked_softmax(s_w, (dist >= 0) & (dist < NSA_WINDOW) & (kw_pos[None, :] >= 0))
    o_w = jnp.einsum("bgrtl,blgd->btgrd", p_w.astype(vw.dtype), vw)
    o = (gates[:, :, 0, :, None] * o_c.reshape(B, Tq, N_HEADS, HEAD_DIM)
         + gates[:, :, 1, :, None] * o_s.reshape(B, Tq, N_HEADS, HEAD_DIM)
         + gates[:, :, 2, :, None] * o_w.reshape(B, Tq, N_HEADS, HEAD_DIM))
    return o.reshape(B, Tq, ATTN_WIDTH)


def nsa_prompt(x, w_in, b_gate, w_cmp, w_o):
    B, T, _ = x.shape
    pos = jnp.arange(T)
    q_raw, q_rot, kc, vc, ks, vs, kw, vw, gates = nsa_project(x, w_in, b_gate, pos)
    kc_sum, vc_sum = block_summaries(kc, w_cmp), block_summaries(vc, w_cmp)
    front = ((0, 0), (NSA_WINDOW, 0), (0, 0), (0, 0))
    kw_pad, vw_pad = jnp.pad(kw, front), jnp.pad(vw, front)

    def block(i):
        q0 = i * Q_BLOCK
        sl = lambda a, n: lax.dynamic_slice_in_dim(a, q0, n, 1)
        return nsa_block(sl(q_raw, Q_BLOCK), sl(q_rot, Q_BLOCK), sl(gates, Q_BLOCK), q0 + jnp.arange(Q_BLOCK),
                         kc_sum, vc_sum, ks, vs,
                         sl(kw_pad, NSA_WINDOW + Q_BLOCK), sl(vw_pad, NSA_WINDOW + Q_BLOCK),
                         q0 - NSA_WINDOW + jnp.arange(NSA_WINDOW + Q_BLOCK))

    o = lax.map(block, jnp.arange(T // Q_BLOCK))
    o = jnp.swapaxes(o, 0, 1).reshape(B, T, ATTN_WIDTH)
    keep = min(NSA_WINDOW, T)
    return o @ w_o, (kc, vc, ks, vs, kw[:, T - keep:], vw[:, T - keep:])


def nsa_sample(x, layer, cmp_k, cmp_v, slc_k, slc_v, win_k, win_v, page_table, w_in, b_gate, w_cmp, w_o):
    _, T, _ = x.shape
    past = page_table.shape[1] * PAGE_SIZE
    pos = past + jnp.arange(T)
    q_raw, q_rot, kc, vc, ks, vs, kw, vw, gates = nsa_project(x, w_in, b_gate, pos)
    cat = lambda pool, new: jnp.concatenate([gather_pages(pool, layer, page_table), new], axis=1)
    nb = -(-(past + T) // NSA_BLOCK)
    ks_all = pad_rows(cat(slc_k, ks), nb * NSA_BLOCK)
    vs_all = pad_rows(cat(slc_v, vs), nb * NSA_BLOCK)
    kw_all = jnp.concatenate([win_k[layer], kw], axis=1)
    vw_all = jnp.concatenate([win_v[layer], vw], axis=1)
    wb = win_k.shape[2]
    o = nsa_block(q_raw, q_rot, gates, pos,
                  block_summaries(cat(cmp_k, kc), w_cmp), block_summaries(cat(cmp_v, vc), w_cmp),
                  ks_all, vs_all, kw_all, vw_all, past - wb + jnp.arange(wb + T))
    keep = min(NSA_WINDOW, wb + T)
    return o @ w_o, (kc, vc, ks, vs, kw_all[:, wb + T - keep:], vw_all[:, wb + T - keep:])


def setup_inputs(seed: int = 0) -> dict:
    keys = iter(jax.random.split(jax.random.key(seed), 32))
    f32 = jnp.float32

    def nrm(shape, scale=1.0):
        return scale * jax.random.normal(next(keys), shape, f32)

    n_pages = PAST_LEN // PAGE_SIZE
    n_used = DEC_BATCH * n_pages
    n_phys = n_used + max(1, n_used // 4)
    w_buf = min(NSA_WINDOW, PAST_LEN)
    fox_pool = (N_FOX, n_phys, PAGE_SIZE, N_HEADS, HEAD_DIM)
    nsa_pool = (N_NSA, n_phys, PAGE_SIZE, NSA_KV_HEADS, HEAD_DIM)
    win = (N_NSA, DEC_BATCH, w_buf, NSA_KV_HEADS, HEAD_DIM)
    page_table = jax.random.permutation(next(keys), n_phys)[:n_used].reshape(DEC_BATCH, n_pages).astype(jnp.int32)
    return {
        "x_prompt": nrm((BATCH, SEQ, D_MODEL)),
        "x_sample": nrm((DEC_BATCH, DEC_SEQ, D_MODEL)),
        "cache_fox_k": nrm(fox_pool),
        "cache_fox_v": nrm(fox_pool),
        "cache_fox_logf": jax.nn.log_sigmoid(3.5 + nrm(fox_pool[:4], 1.5)),
        "cache_nsa_cmp_k": nrm(nsa_pool),
        "cache_nsa_cmp_v": nrm(nsa_pool),
        "cache_nsa_slc_k": nrm(nsa_pool),
        "cache_nsa_slc_v": nrm(nsa_pool),
        "state_nsa_win_k": nrm(win),
        "state_nsa_win_v": nrm(win),
        "page_table": page_table,
        "fox_w_in": nrm((N_FOX, D_MODEL, FOX_IN), D_MODEL ** -0.5),
        "fox_b_f": jax.random.uniform(next(keys), (N_FOX, N_HEADS), f32, 1.0, 6.0),
        "fox_w_o": nrm((N_FOX, ATTN_WIDTH, D_MODEL), DN_BETA * ATTN_WIDTH ** -0.5),
        "nsa_w_in": nrm((N_NSA, D_MODEL, NSA_IN), D_MODEL ** -0.5),
        "nsa_b_gate": nrm((N_NSA, 3 * N_HEADS), 0.1),
        "nsa_w_cmp": (1.0 + nrm((N_NSA, NSA_BLOCK, NSA_KV_HEADS), 0.1)) / NSA_BLOCK,
        "nsa_w_o": nrm((N_NSA, ATTN_WIDTH, D_MODEL), DN_BETA * ATTN_WIDTH ** -0.5),
        "ffn_w_gu": nrm((DEPTH, D_MODEL, 2 * D_FF), D_MODEL ** -0.5),
        "ffn_w_down": nrm((DEPTH, D_FF, D_MODEL), DN_BETA * D_FF ** -0.5),
        "ln_mix_g": 1.0 + nrm((DEPTH, D_MODEL), 0.05),
        "ln_mix_b": nrm((DEPTH, D_MODEL), 0.02),
        "ln_ffn_g": 1.0 + nrm((DEPTH, D_MODEL), 0.05),
        "ln_ffn_b": nrm((DEPTH, D_MODEL), 0.02),
    }


def reference(x_prompt, x_sample, cache_fox_k, cache_fox_v, cache_fox_logf,
              cache_nsa_cmp_k, cache_nsa_cmp_v, cache_nsa_slc_k, cache_nsa_slc_v,
              state_nsa_win_k, state_nsa_win_v, page_table,
              fox_w_in, fox_b_f, fox_w_o, nsa_w_in, nsa_b_gate, nsa_w_cmp, nsa_w_o,
              ffn_w_gu, ffn_w_down, ln_mix_g, ln_mix_b, ln_ffn_g, ln_ffn_b):
    yp, ys = x_prompt, x_sample
    fox_new_p, fox_new_s, nsa_new_p, nsa_new_s = [], [], [], []
    for i in range(DEPTH):
        j = i // N_MIXERS
        if i % N_MIXERS == 0:
            hp, new_p = fox_prompt(yp, fox_w_in[j], fox_b_f[j], fox_w_o[j])
            hs, new_s = fox_sample(ys, j, cache_fox_k, cache_fox_v, cache_fox_logf, page_table,
                                   fox_w_in[j], fox_b_f[j], fox_w_o[j])
            fox_new_p.append(new_p)
            fox_new_s.append(new_s)
        else:
            hp, new_p = nsa_prompt(yp, nsa_w_in[j], nsa_b_gate[j], nsa_w_cmp[j], nsa_w_o[j])
            hs, new_s = nsa_sample(ys, j, cache_nsa_cmp_k, cache_nsa_cmp_v, cache_nsa_slc_k, cache_nsa_slc_v,
                                   state_nsa_win_k, state_nsa_win_v, page_table,
                                   nsa_w_in[j], nsa_b_gate[j], nsa_w_cmp[j], nsa_w_o[j])
            nsa_new_p.append(new_p)
            nsa_new_s.append(new_s)
        yp = post_norm(yp, hp, ln_mix_g[i], ln_mix_b[i])
        ys = post_norm(ys, hs, ln_mix_g[i], ln_mix_b[i])
        yp = post_norm(yp, swiglu(yp, ffn_w_gu[i], ffn_w_down[i]), ln_ffn_g[i], ln_ffn_b[i])
        ys = post_norm(ys, swiglu(ys, ffn_w_gu[i], ffn_w_down[i]), ln_ffn_g[i], ln_ffn_b[i])
    fp = [jnp.stack(a) for a in zip(*fox_new_p)]
    fs = [jnp.stack(a) for a in zip(*fox_new_s)]
    sp = [jnp.stack(a) for a in zip(*nsa_new_p)]
    ss = [jnp.stack(a) for a in zip(*nsa_new_s)]
    return (yp, ys, fp[0], fs[0], fp[1], fs[1], fp[2], fs[2],
            sp[0], ss[0], sp[1], ss[1], sp[2], ss[2], sp[3], ss[3], sp[4], ss[4], sp[5], ss[5])
```

```python
import functools

import jax
import jax.numpy as jnp
from jax import lax
from jax.experimental import pallas as pl
from jax.experimental.pallas import tpu as pltpu

F32 = jnp.float32
BF16 = jnp.bfloat16

LANES = 128
HEAD_DIM = 64
NSA_KV_HEADS = 4
NSA_GROUP = 4
NSA_BLOCK = 64
NSA_TOP_N = 16
NSA_WINDOW = 512
PAGE_SIZE = 128
ROPE_THETA = 10000.0
LN_EPS = 1e-5
NEG_BIG = -1e30
SEL_FORCE = 1e9
VMEM_LIMIT_BYTES = 56 * 1024 * 1024

TOKEN_TILE = 512
FOX_TILE = 256
NSA_Q_TILE = 128
NSA_K_TILE = 256
DEC_PAGES_PER_STEP = 8


def _dot(a, b):
    return jnp.dot(a, b, preferred_element_type=F32)


def _dot_nt(a, b):
    return lax.dot_general(a, b, (((1,), (1,)), ((), ())), preferred_element_type=F32)


def _params(*sem):
    return pltpu.CompilerParams(dimension_semantics=sem, vmem_limit_bytes=VMEM_LIMIT_BYTES)


def _layer_norm(x, g, b):
    mu = jnp.mean(x, axis=-1, keepdims=True)
    xc = x - mu
    var = jnp.mean(xc * xc, axis=-1, keepdims=True)
    return xc * lax.rsqrt(var + LN_EPS) * g + b


def _full(shape):
    n = len(shape)
    return pl.BlockSpec(shape, lambda *_: (0,) * n)


def _fox_proj_kernel(x_ref, w_ref, wf_ref, bf_ref, q_ref, k_ref, v_ref, kb_ref, vb_ref, lf_ref):
    aw = q_ref.shape[1]
    x = x_ref[...].astype(BF16)
    q_ref[...] = _dot(x, w_ref[:, 0:aw]).astype(BF16)
    k = _dot(x, w_ref[:, aw:2 * aw])
    k_ref[...] = k
    kb_ref[...] = k.astype(BF16)
    v = _dot(x, w_ref[:, 2 * aw:3 * aw])
    v_ref[...] = v
    vb_ref[...] = v.astype(BF16)
    z = _dot(x, wf_ref[...]) + bf_ref[...]
    lf = jnp.minimum(z, 0.0) - jnp.log1p(jnp.exp(-jnp.abs(z)))
    lf_ref[...] = lf[:, :lf_ref.shape[1]]


def fox_project(x, w_in, b_f):
    m, d = x.shape
    nh = b_f.shape[0]
    aw = (w_in.shape[1] - nh) // 3
    scale = HEAD_DIM ** -0.5
    w_qkv = jnp.concatenate([w_in[:, :aw] * scale, w_in[:, aw:3 * aw]], axis=1).astype(BF16)
    w_f = jnp.pad(w_in[:, 3 * aw:], ((0, 0), (0, LANES - nh))).astype(BF16)
    b = jnp.pad(b_f, (0, LANES - nh)).reshape(1, LANES)
    tm = min(TOKEN_TILE, m)
    row = lambda w: pl.BlockSpec((tm, w), lambda i: (i, 0))
    return pl.pallas_call(
        _fox_proj_kernel,
        grid=(m // tm,),
        in_specs=[row(d), _full(w_qkv.shape), _full(w_f.shape), _full(b.shape)],
        out_specs=[row(aw), row(aw), row(aw), row(aw), row(aw), row(nh)],
        out_shape=[jax.ShapeDtypeStruct((m, aw), BF16), jax.ShapeDtypeStruct((m, aw), F32),
                   jax.ShapeDtypeStruct((m, aw), F32), jax.ShapeDtypeStruct((m, aw), BF16),
                   jax.ShapeDtypeStruct((m, aw), BF16), jax.ShapeDtypeStruct((m, nh), F32)],
        compiler_params=_params("parallel"),
        name="fox_proj",
    )(x, w_qkv, w_f, b)


def _rope(x, cos, sin_signed, lo_half):
    outs = []
    for c in range(x.shape[1] // LANES):
        xc = x[:, c * LANES:(c + 1) * LANES]
        swapped = jnp.where(lo_half, pltpu.roll(xc, LANES - HEAD_DIM // 2, 1),
                            pltpu.roll(xc, HEAD_DIM // 2, 1))
        outs.append(xc * cos + swapped * sin_signed)
    return outs


def _nsa_proj_kernel(x_ref, w_ref, bg_ref, wc_ref, cos_ref, sin_ref,
                     qraw_ref, qrot_ref, kc_ref, vc_ref, ks_ref, vs_ref, kw_ref, vw_ref,
                     ksd_ref, vsd_ref, kwd_ref, vwd_ref, kcsd_ref, vcsd_ref, gate_ref):
    aw = qraw_ref.shape[1]
    kvw = kc_ref.shape[1]
    dw = ksd_ref.shape[1]
    tm = x_ref.shape[0]
    x = x_ref[...].astype(BF16)
    cos = cos_ref[...]
    sin = sin_ref[...]
    lane = lax.broadcasted_iota(jnp.int32, (tm, LANES), 1)
    lo_half = (lane % HEAD_DIM) < (HEAD_DIM // 2)

    def rope_store(val, refs):
        for c, piece in enumerate(_rope(val, cos, sin, lo_half)):
            for r in refs:
                r[:, c * LANES:(c + 1) * LANES] = piece.astype(r.dtype)

    q = _dot(x, w_ref[:, 0:aw])
    qraw_ref[...] = q.astype(BF16)
    rope_store(q, [qrot_ref])
    off = aw
    wc = wc_ref[...]
    nblk = tm // NSA_BLOCK

    def summary(val, out_ref):
        s = jnp.sum(val.reshape(nblk, NSA_BLOCK, kvw) * wc[None], axis=1)
        for g in range(NSA_KV_HEADS):
            sg = s[:, g * HEAD_DIM:(g + 1) * HEAD_DIM].astype(out_ref.dtype)
            out_ref[:, g * LANES:g * LANES + HEAD_DIM] = sg
            out_ref[:, g * LANES + HEAD_DIM:(g + 1) * LANES] = sg

    kc = _dot(x, w_ref[:, off:off + kvw])
    kc_ref[...] = kc
    summary(kc, kcsd_ref)
    vc = _dot(x, w_ref[:, off + kvw:off + 2 * kvw])
    vc_ref[...] = vc
    summary(vc, vcsd_ref)
    rope_store(_dot(x, w_ref[:, off + 2 * kvw:off + 3 * kvw]), [ks_ref])
    vs_ref[...] = _dot(x, w_ref[:, off + 3 * kvw:off + 4 * kvw])
    rope_store(_dot(x, w_ref[:, off + 4 * kvw:off + 5 * kvw]), [kw_ref])
    vw_ref[...] = _dot(x, w_ref[:, off + 5 * kvw:off + 6 * kvw])
    off += 6 * kvw
    rope_store(_dot(x, w_ref[:, off:off + dw]), [ksd_ref])
    vsd_ref[...] = _dot(x, w_ref[:, off + dw:off + 2 * dw]).astype(BF16)
    rope_store(_dot(x, w_ref[:, off + 2 * dw:off + 3 * dw]), [kwd_ref])
    vwd_ref[...] = _dot(x, w_ref[:, off + 3 * dw:off + 4 * dw]).astype(BF16)
    off += 4 * dw
    z = _dot(x, w_ref[:, off:off + LANES]) + bg_ref[...]
    gate = 1.0 / (1.0 + jnp.exp(-z))
    gate_ref[...] = gate[:, :gate_ref.shape[1]]


def _dup_heads(w):
    d, n = w.shape
    g = n // HEAD_DIM
    w = w.reshape(d, g, 1, HEAD_DIM)
    return jnp.broadcast_to(w, (d, g, 2, HEAD_DIM)).reshape(d, g * LANES)


def nsa_project(x, w_in, b_gate, w_cmp, cos_t, sin_t):
    m, d = x.shape
    ng = b_gate.shape[0]
    aw = (ng // 3) * HEAD_DIM
    kvw = NSA_KV_HEADS * HEAD_DIM
    dw = NSA_KV_HEADS * LANES
    scale = HEAD_DIM ** -0.5
    kv = w_in[:, aw:aw + 6 * kvw]
    cols = [w_in[:, :aw] * scale, kv]
    for i in (2, 3, 4, 5):
        cols.append(_dup_heads(kv[:, i * kvw:(i + 1) * kvw]))
    cols.append(jnp.pad(w_in[:, aw + 6 * kvw:], ((0, 0), (0, LANES - ng))))
    w_all = jnp.concatenate(cols, axis=1).astype(BF16)
    bg = jnp.pad(b_gate, (0, LANES - ng)).reshape(1, LANES)
    wc = jnp.repeat(w_cmp, HEAD_DIM, axis=1)
    tm = min(TOKEN_TILE, m)
    row = lambda w: pl.BlockSpec((tm, w), lambda i: (i, 0))
    sds = jax.ShapeDtypeStruct
    nblk = tm // NSA_BLOCK
    outs = pl.pallas_call(
        _nsa_proj_kernel,
        grid=(m // tm,),
        in_specs=[row(d), _full(w_all.shape), _full(bg.shape), _full(wc.shape), row(LANES), row(LANES)],
        out_specs=[row(aw), row(aw)] + [row(kvw)] * 6 + [row(dw)] * 4
                  + [pl.BlockSpec((nblk, dw), lambda i: (i, 0))] * 2 + [row(ng)],
        out_shape=[sds((m, aw), BF16), sds((m, aw), BF16)] + [sds((m, kvw), F32)] * 6
                  + [sds((m, dw), BF16)] * 4 + [sds((m // NSA_BLOCK, dw), BF16)] * 2 + [sds((m, ng), F32)],
        compiler_params=_params("parallel"),
        name="nsa_proj",
    )(x, w_all, bg, wc, cos_t, sin_t)
    names = ("qraw", "qrot", "kc", "vc", "ks", "vs", "kw", "vw", "ksd", "vsd", "kwd", "vwd", "kcsd", "vcsd", "gate")
    return dict(zip(names, outs))


def _wo_ln_kernel(x_ref, *rest, gated, alpha):
    if gated:
        oc, os_, ow, gc, gs, gw, w_ref, g_ref, b_ref, y_ref = rest
        o = gc[...] * oc[...] + gs[...] * os_[...] + gw[...] * ow[...]
    else:
        o_ref, w_ref, g_ref, b_ref, y_ref = rest
        o = o_ref[...]
    h = _dot(o.astype(BF16), w_ref[...])
    y_ref[...] = _layer_norm(alpha * x_ref[...] + h, g_ref[...], b_ref[...])


def wo_post_norm(x, branches, w_o, g, b, alpha):
    m, d = x.shape
    aw = w_o.shape[0]
    tm = min(TOKEN_TILE, m)
    row = lambda w: pl.BlockSpec((tm, w), lambda i: (i, 0))
    return pl.pallas_call(
        functools.partial(_wo_ln_kernel, gated=len(branches) > 1, alpha=alpha),
        grid=(m // tm,),
        in_specs=[row(d)] + [row(aw)] * len(branches) + [_full(w_o.shape), _full((1, d)), _full((1, d))],
        out_specs=row(d),
        out_shape=jax.ShapeDtypeStruct((m, d), F32),
        compiler_params=_params("parallel"),
        name="wo_post_norm",
    )(x, *branches, w_o.astype(BF16), g.reshape(1, d), b.reshape(1, d))


def _ffn_kernel(x_ref, wg_ref, wu_ref, wd_ref, g_ref, b_ref, y_ref, acc_ref, *, alpha):
    f = pl.program_id(1)
    x = x_ref[...]
    xb = x.astype(BF16)
    gate = _dot(xb, wg_ref[...])
    up = _dot(xb, wu_ref[...])
    h = gate / (1.0 + jnp.exp(-gate)) * up
    part = _dot(h.astype(BF16), wd_ref[...])

    @pl.when(f == 0)
    def _():
        acc_ref[...] = part

    @pl.when(f > 0)
    def _():
        acc_ref[...] += part

    @pl.when(f == pl.num_programs(1) - 1)
    def _():
        y_ref[...] = _layer_norm(alpha * x + acc_ref[...], g_ref[...], b_ref[...])


def ffn_post_norm(x, w_gu, w_down, g, b, alpha):
    m, d = x.shape
    dff = w_down.shape[0]
    nf = 2
    tf = dff // nf
    tm = min(TOKEN_TILE, m)
    w_gu = w_gu.astype(BF16)
    return pl.pallas_call(
        functools.partial(_ffn_kernel, alpha=alpha),
        grid=(m // tm, nf),
        in_specs=[pl.BlockSpec((tm, d), lambda i, f: (i, 0)),
                  pl.BlockSpec((d, tf), lambda i, f: (0, f)),
                  pl.BlockSpec((d, tf), lambda i, f: (0, nf + f)),
                  pl.BlockSpec((tf, d), lambda i, f: (f, 0)),
                  pl.BlockSpec((1, d), lambda i, f: (0, 0)),
                  pl.BlockSpec((1, d), lambda i, f: (0, 0))],
        out_specs=pl.BlockSpec((tm, d), lambda i, f: (i, 0)),
        out_shape=jax.ShapeDtypeStruct((m, d), F32),
        scratch_shapes=[pltpu.VMEM((tm, d), F32)],
        compiler_params=_params("parallel", "arbitrary"),
        name="ffn_post_norm",
    )(x, w_gu, w_gu, w_down.astype(BF16), g.reshape(1, d), b.reshape(1, d))


def _stack_heads(q_ref, qs_ref, tq):
    lane = lax.broadcasted_iota(jnp.int32, (tq, LANES), 1)
    lo = lane < HEAD_DIM
    for c in range(q_ref.shape[1] // LANES):
        qc = q_ref[:, c * LANES:(c + 1) * LANES].astype(F32)
        qs_ref[(2 * c) * tq:(2 * c + 1) * tq, :] = jnp.where(lo, qc, 0.0).astype(qs_ref.dtype)
        qs_ref[(2 * c + 1) * tq:(2 * c + 2) * tq, :] = jnp.where(lo, 0.0, qc).astype(qs_ref.dtype)


def _unstack_heads(vals, o_ref):
    tq = vals.shape[1]
    lane = lax.broadcasted_iota(jnp.int32, (tq, LANES), 1)
    lo = lane < HEAD_DIM
    for c in range(vals.shape[0] // 2):
        o_ref[:, c * LANES:(c + 1) * LANES] = jnp.where(lo, vals[2 * c], vals[2 * c + 1])


def _softmax_init(m_ref, l_ref, acc_ref):
    m_ref[...] = jnp.full(m_ref.shape, NEG_BIG, F32)
    l_ref[...] = jnp.zeros(l_ref.shape, F32)
    acc_ref[...] = jnp.zeros(acc_ref.shape, F32)


def _softmax_update(s3, v_tile, m_ref, l_ref, acc_ref):
    n, tq, tk = s3.shape
    m_prev = m_ref[...]
    m_new = jnp.maximum(m_prev, jnp.max(s3, axis=-1, keepdims=True))
    alpha = jnp.exp(m_prev - m_new)
    p = jnp.exp(s3 - m_new)
    l_ref[...] = alpha * l_ref[...] + jnp.sum(p, axis=-1, keepdims=True)
    pv = _dot(p.reshape(n * tq, tk).astype(BF16), v_tile)
    acc_ref[...] = alpha * acc_ref[...] + pv.reshape(n, tq, LANES)
    m_ref[...] = m_new


def _fox_attn_kernel(q_ref, k_ref, v_ref, f_ref, o_ref, qs_ref, m_ref, l_ref, acc_ref, *, t):
    qi = pl.program_id(2)
    _stack_heads(q_ref, qs_ref, t)
    _softmax_init(m_ref, l_ref, acc_ref)
    q0 = pl.multiple_of(qi * t, t)
    f_q0 = f_ref[0, 0, :, pl.ds(q0, LANES)][:, 0:1]

    def tile(kt, causal):
        k0 = pl.multiple_of(kt * t, t)
        s = _dot_nt(qs_ref[...], k_ref[pl.ds(k0, t), :])
        bias = f_q0 - f_ref[0, 0, :, pl.ds(k0, t)]
        s3 = s.reshape(2, t, t) + bias[:, None, :]
        if causal:
            row = lax.broadcasted_iota(jnp.int32, (t, t), 0)
            col = lax.broadcasted_iota(jnp.int32, (t, t), 1)
            s3 = jnp.where((col <= row)[None], s3, NEG_BIG)
        _softmax_update(s3, v_ref[pl.ds(k0, t), :], m_ref, l_ref, acc_ref)

    def body(kt, carry):
        tile(kt, False)
        return carry

    lax.fori_loop(0, qi, body, 0)
    tile(qi, True)
    _unstack_heads(acc_ref[...] / l_ref[...], o_ref)


def fox_prompt_attention(q, k, v, f_cum, batch):
    m, aw = q.shape
    seq = m // batch
    t = FOX_TILE
    nq = seq // t
    npair = aw // LANES
    f = jnp.swapaxes(f_cum, 1, 2).reshape(batch, npair, 2, seq)
    return pl.pallas_call(
        functools.partial(_fox_attn_kernel, t=t),
        grid=(batch, npair, nq),
        in_specs=[pl.BlockSpec((t, LANES), lambda b, c, i: (b * nq + i, c)),
                  pl.BlockSpec((seq, LANES), lambda b, c, i: (b, c)),
                  pl.BlockSpec((seq, LANES), lambda b, c, i: (b, c)),
                  pl.BlockSpec((1, 1, 2, seq), lambda b, c, i: (b, c, 0, 0))],
        out_specs=pl.BlockSpec((t, LANES), lambda b, c, i: (b * nq + i, c)),
        out_shape=jax.ShapeDtypeStruct((m, aw), F32),
        scratch_shapes=[pltpu.VMEM((2 * t, LANES), BF16), pltpu.VMEM((2, t, 1), F32),
                        pltpu.VMEM((2, t, 1), F32), pltpu.VMEM((2, t, LANES), F32)],
        compiler_params=_params("parallel", "parallel", "arbitrary"),
        name="fox_prompt_attn",
    )(q, k, v, f)


def _top_n_mask(score, blk_f, n_sel):
    sel = jnp.zeros(score.shape, F32)
    for _ in range(n_sel):
        mx = jnp.max(score, axis=-1, keepdims=True)
        idx = jnp.min(jnp.where(score == mx, blk_f, SEL_FORCE), axis=-1, keepdims=True)
        hit = blk_f == idx
        sel = jnp.where(hit, 1.0, sel)
        score = jnp.where(hit, -jnp.inf, score)
    return sel


def _block_scores(pn_sum, blk, qpos):
    cur = qpos // NSA_BLOCK
    forced = (blk == 0) | (blk == cur) | (blk == cur - 1)
    return jnp.where(blk <= cur, jnp.where(forced, SEL_FORCE, pn_sum), -SEL_FORCE)


def _masked_softmax(s, vis):
    s = jnp.where(vis, s, NEG_BIG)
    p = jnp.where(vis, jnp.exp(s - jnp.max(s, axis=-1, keepdims=True)), 0.0)
    return p / jnp.maximum(jnp.sum(p, axis=-1, keepdims=True), 1e-30)


def _nsa_cmp_kernel(q_ref, kk_ref, vv_ref, oc_ref, sel_ref, qs_ref, *, tq):
    qi = pl.program_id(2)
    nb = kk_ref.shape[0]
    _stack_heads(q_ref, qs_ref, tq)
    s3 = _dot_nt(qs_ref[...], kk_ref[...]).reshape(NSA_GROUP, tq, nb)
    blk = lax.broadcasted_iota(jnp.int32, (tq, nb), 1)
    qpos = qi * tq + lax.broadcasted_iota(jnp.int32, (tq, nb), 0)
    vis = (blk + 1) * NSA_BLOCK - 1 <= qpos
    pn = _masked_softmax(s3, vis[None])
    oc = _dot(pn.reshape(NSA_GROUP * tq, nb).astype(BF16), vv_ref[...])
    _unstack_heads(oc.reshape(NSA_GROUP, tq, LANES), oc_ref)
    score = _block_scores(jnp.sum(pn, axis=0), blk, qpos)
    sel_ref[0, 0] = _top_n_mask(score, blk.astype(F32), min(NSA_TOP_N, nb)).astype(sel_ref.dtype)


def nsa_prompt_compress_select(qraw, kcsd, vcsd, batch):
    m, aw = qraw.shape
    seq = m // batch
    nb = seq // NSA_BLOCK
    assert nb <= LANES, "block axis is mapped onto one vreg lane row"
    tq = NSA_Q_TILE
    nq = seq // tq
    gw = NSA_GROUP * HEAD_DIM
    ng = aw // gw
    return pl.pallas_call(
        functools.partial(_nsa_cmp_kernel, tq=tq),
        grid=(batch, ng, nq),
        in_specs=[pl.BlockSpec((tq, gw), lambda b, g, i: (b * nq + i, g)),
                  pl.BlockSpec((nb, LANES), lambda b, g, i: (b, g)),
                  pl.BlockSpec((nb, LANES), lambda b, g, i: (b, g))],
        out_specs=[pl.BlockSpec((tq, gw), lambda b, g, i: (b * nq + i, g)),
                   pl.BlockSpec((1, 1, tq, nb), lambda b, g, i: (b, g, i, 0))],
        out_shape=[jax.ShapeDtypeStruct((m, aw), F32), jax.ShapeDtypeStruct((batch, ng, seq, nb), BF16)],
        scratch_shapes=[pltpu.VMEM((NSA_GROUP * tq, LANES), BF16)],
        compiler_params=_params("parallel", "parallel", "parallel"),
        name="nsa_prompt_cmp_select",
    )(qraw, kcsd, vcsd)


def _nsa_sel_kernel(q_ref, k_ref, v_ref, sel_ref, e_ref, o_ref, qs_ref, m_ref, l_ref, acc_ref, *, tq, tk):
    qi = pl.program_id(2)
    _stack_heads(q_ref, qs_ref, tq)
    _softmax_init(m_ref, l_ref, acc_ref)
    sel = sel_ref[0, 0]
    q0 = qi * tq
    kt_diag = q0 // tk

    def tile(kt, causal):
        k0 = pl.multiple_of(kt * tk, tk)
        s = _dot_nt(qs_ref[...], k_ref[pl.ds(k0, tk), :])
        chosen = _dot(sel, e_ref[:, pl.ds(k0, tk)])
        s3 = s.reshape(NSA_GROUP, tq, tk) + ((chosen - 1.0) * (-NEG_BIG))[None]
        if causal:
            qpos = q0 + lax.broadcasted_iota(jnp.int32, (tq, tk), 0)
            kpos = k0 + lax.broadcasted_iota(jnp.int32, (tq, tk), 1)
            s3 = jnp.where((kpos <= qpos)[None], s3, NEG_BIG)
        _softmax_update(s3, v_ref[pl.ds(k0, tk), :], m_ref, l_ref, acc_ref)

    def body(kt, carry):
        tile(kt, False)
        return carry

    lax.fori_loop(0, kt_diag, body, 0)
    tile(kt_diag, True)
    _unstack_heads(acc_ref[...] / l_ref[...], o_ref)


def nsa_prompt_selected(qrot, ksd, vsd, sel, batch):
    m, aw = qrot.shape
    seq = m // batch
    nb = seq // NSA_BLOCK
    tq, tk = NSA_Q_TILE, NSA_K_TILE
    nq = seq // tq
    gw = NSA_GROUP * HEAD_DIM
    ng = aw // gw
    expand = (jnp.arange(nb)[:, None] == jnp.arange(seq)[None, :] // NSA_BLOCK).astype(BF16)
    return pl.pallas_call(
        functools.partial(_nsa_sel_kernel, tq=tq, tk=tk),
        grid=(batch, ng, nq),
        in_specs=[pl.BlockSpec((tq, gw), lambda b, g, i: (b * nq + i, g)),
                  pl.BlockSpec((seq, LANES), lambda b, g, i: (b, g)),
                  pl.BlockSpec((seq, LANES), lambda b, g, i: (b, g)),
                  pl.BlockSpec((1, 1, tq, nb), lambda b, g, i: (b, g, i, 0)),
                  _full((nb, seq))],
        out_specs=pl.BlockSpec((tq, gw), lambda b, g, i: (b * nq + i, g)),
        out_shape=jax.ShapeDtypeStruct((m, aw), F32),
        scratch_shapes=[pltpu.VMEM((NSA_GROUP * tq, LANES), BF16), pltpu.VMEM((NSA_GROUP, tq, 1), F32),
                        pltpu.VMEM((NSA_GROUP, tq, 1), F32), pltpu.VMEM((NSA_GROUP, tq, LANES), F32)],
        compiler_params=_params("parallel", "parallel", "arbitrary"),
        name="nsa_prompt_selected",
    )(qrot, ksd, vsd, sel, expand)


def _nsa_win_kernel(q_ref, k_ref, v_ref, o_ref, qs_ref, m_ref, l_ref, acc_ref, *, tq):
    qi = pl.program_id(2)
    _stack_heads(q_ref, qs_ref, tq)
    _softmax_init(m_ref, l_ref, acc_ref)
    n_tiles = jnp.minimum(qi, NSA_WINDOW // tq) + 1

    def body(j, carry):
        kt = qi - j
        k0 = pl.multiple_of(kt * tq, tq)
        s = _dot_nt(qs_ref[...], k_ref[pl.ds(k0, tq), :])
        dist = ((qi - kt) * tq + lax.broadcasted_iota(jnp.int32, (tq, tq), 0)
                - lax.broadcasted_iota(jnp.int32, (tq, tq), 1))
        vis = (dist >= 0) & (dist < NSA_WINDOW)
        s3 = jnp.where(vis[None], s.reshape(NSA_GROUP, tq, tq), NEG_BIG)
        _softmax_update(s3, v_ref[pl.ds(k0, tq), :], m_ref, l_ref, acc_ref)
        return carry

    lax.fori_loop(0, n_tiles, body, 0)
    _unstack_heads(acc_ref[...] / l_ref[...], o_ref)


def nsa_prompt_window(qrot, kwd, vwd, batch):
    m, aw = qrot.shape
    seq = m // batch
    tq = NSA_Q_TILE
    nq = seq // tq
    gw = NSA_GROUP * HEAD_DIM
    ng = aw // gw
    return pl.pallas_call(
        functools.partial(_nsa_win_kernel, tq=tq),
        grid=(batch, ng, nq),
        in_specs=[pl.BlockSpec((tq, gw), lambda b, g, i: (b * nq + i, g)),
                  pl.BlockSpec((seq, LANES), lambda b, g, i: (b, g)),
                  pl.BlockSpec((seq, LANES), lambda b, g, i: (b, g))],
        out_specs=pl.BlockSpec((tq, gw), lambda b, g, i: (b * nq + i, g)),
        out_shape=jax.ShapeDtypeStruct((m, aw), F32),
        scratch_shapes=[pltpu.VMEM((NSA_GROUP * tq, LANES), BF16), pltpu.VMEM((NSA_GROUP, tq, 1), F32),
                        pltpu.VMEM((NSA_GROUP, tq, 1), F32), pltpu.VMEM((NSA_GROUP, tq, LANES), F32)],
        compiler_params=_params("parallel", "parallel", "arbitrary"),
        name="nsa_prompt_window",
    )(qrot, kwd, vwd)


def _tile_rows(x, n):
    return jnp.concatenate([x] * n, axis=0)


def _column_update(state, cols, v_rows):
    m_prev, l_prev, acc = state
    m_new = m_prev
    for c in cols:
        m_new = jnp.maximum(m_new, c)
    alpha = jnp.exp(m_prev - m_new)
    l_new = alpha * l_prev
    acc = alpha * acc
    for j, c in enumerate(cols):
        p = jnp.exp(c - m_new)
        l_new = l_new + p
        acc = acc + p * v_rows[j:j + 1, :]
    return m_new, l_new, acc


def _block_update(state, s, v_bf):
    m_prev, l_prev, acc = state
    m_new = jnp.maximum(m_prev, jnp.max(s, axis=-1, keepdims=True))
    alpha = jnp.exp(m_prev - m_new)
    p = jnp.exp(s - m_new)
    l_new = alpha * l_prev + jnp.sum(p, axis=-1, keepdims=True)
    acc = alpha * acc + _dot(p.astype(BF16), v_bf)
    return m_new, l_new, acc


def _fox_dec_kernel(pt_ref, qbd_ref, kn_ref, vn_ref, fp_ref, fn_ref, *rest, pp, n_new, heads):
    k_refs, v_refs = rest[:pp], rest[pp:2 * pp]
    o_ref, m_ref, l_ref, acc_ref = rest[2 * pp:]
    step = pl.program_id(1)
    rows = n_new * heads
    qbd = qbd_ref[0]
    fn = fn_ref[0]
    f_ref0 = fn[:, 0:1]

    @pl.when(step == 0)
    def _():
        qf = qbd.astype(F32)
        kn = kn_ref[0]
        rowq = lax.broadcasted_iota(jnp.int32, (rows, 1), 0) // heads
        cols = []
        for j in range(n_new):
            sj = jnp.sum(qf * kn[j:j + 1, :], axis=-1, keepdims=True)
            sj = sj + _tile_rows(f_ref0 - fn[:, j:j + 1], n_new)
            cols.append(jnp.where(j <= rowq, sj, NEG_BIG))
        init = (jnp.full((rows, 1), NEG_BIG, F32), jnp.zeros((rows, 1), F32),
                jnp.zeros(acc_ref.shape, F32))
        m, l, acc = _column_update(init, cols, vn_ref[0])
        m_ref[...] = m
        l_ref[...] = l
        acc_ref[...] = acc

    state = (m_ref[...], l_ref[...], acc_ref[...])
    for i in range(pp):
        s = _dot_nt(qbd, k_refs[i][0, 0].astype(BF16))
        bias = f_ref0 - fp_ref[0][:, i * PAGE_SIZE:(i + 1) * PAGE_SIZE]
        state = _block_update(state, s + _tile_rows(bias, n_new), v_refs[i][0, 0].astype(BF16))
    m_ref[...], l_ref[...], acc_ref[...] = state

    @pl.when(step == pl.num_programs(1) - 1)
    def _():
        w = acc_ref.shape[1]
        o = acc_ref[...] / l_ref[...]
        own = (lax.broadcasted_iota(jnp.int32, (rows, w), 0) % heads
               == lax.broadcasted_iota(jnp.int32, (rows, w), 1) // HEAD_DIM)
        o_ref[0] = jnp.sum(jnp.where(own, o, 0.0).reshape(n_new, heads, w), axis=1)


def _block_diag_queries(q, n_lane_groups, order):
    bsz, tq, h, dh = q.shape
    r = h // n_lane_groups
    onehot = (jnp.arange(h)[:, None] // r == jnp.arange(n_lane_groups)[None, :]).astype(q.dtype)
    x = q[:, :, :, None, :] * onehot[None, None, :, :, None]
    if order == "rqg":
        x = x.reshape(bsz, tq, n_lane_groups, r, n_lane_groups, dh).transpose(0, 3, 1, 2, 4, 5)
    return x.reshape(bsz, tq * h, n_lane_groups * dh)


def fox_decode_attention(q, k_new, v_new, f_past, f_new, cache_k, cache_v, layer, page_table):
    bsz, n_new, aw = q.shape
    heads = aw // HEAD_DIM
    n_pages = page_table.shape[1]
    pp = min(DEC_PAGES_PER_STEP, n_pages)
    rows = n_new * heads
    qbd = _block_diag_queries(q.reshape(bsz, n_new, heads, HEAD_DIM), heads, "qh")
    fn = jnp.pad(f_new, ((0, 0), (0, 0), (0, LANES - n_new)))
    page = lambda i: pl.BlockSpec(
        (1, 1, PAGE_SIZE, aw), lambda b, s, pt: (layer, pt[b * n_pages + s * pp + i], 0, 0))
    per_b = lambda shape: pl.BlockSpec((1,) + shape, lambda b, s, pt: (b, 0, 0))
    grid_spec = pltpu.PrefetchScalarGridSpec(
        num_scalar_prefetch=1,
        grid=(bsz, n_pages // pp),
        in_specs=[per_b((rows, aw)), per_b((n_new, aw)), per_b((n_new, aw)),
                  pl.BlockSpec((1, heads, pp * PAGE_SIZE), lambda b, s, pt: (b, 0, s)),
                  per_b((heads, LANES))]
                 + [page(i) for i in range(pp)] * 2,
        out_specs=per_b((n_new, aw)),
        scratch_shapes=[pltpu.VMEM((rows, 1), F32), pltpu.VMEM((rows, 1), F32), pltpu.VMEM((rows, aw), F32)],
    )
    return pl.pallas_call(
        functools.partial(_fox_dec_kernel, pp=pp, n_new=n_new, heads=heads),
        grid_spec=grid_spec,
        out_shape=jax.ShapeDtypeStruct((bsz, n_new, aw), F32),
        compiler_params=_params("parallel", "arbitrary"),
        name="fox_decode_attn",
    )(page_table.reshape(-1), qbd, k_new, v_new, f_past, fn, *([cache_k] * pp), *([cache_v] * pp))


def _fold_own_group(acc, n_groups):
    rows, w = acc.shape
    own = (lax.broadcasted_iota(jnp.int32, (rows, w), 0) % n_groups
           == lax.broadcasted_iota(jnp.int32, (rows, w), 1) // HEAD_DIM)
    a = jnp.where(own, acc, 0.0)
    out = a[:, 0:HEAD_DIM]
    for g in range(1, n_groups):
        out = out + a[:, g * HEAD_DIM:(g + 1) * HEAD_DIM]
    return out


def _nsa_dec_cmp_kernel(pt_ref, qbd_ref, kn_ref, vn_ref, w2_ref, *rest, n_pages, n_new, past):
    k_refs, v_refs = rest[:n_pages], rest[n_pages:2 * n_pages]
    oc_ref, sel_ref, kcs_ref, vcs_ref = rest[2 * n_pages:]
    g = NSA_KV_HEADS
    rows = qbd_ref.shape[1]
    w2 = w2_ref[...]
    kcs_ref[...] = jnp.zeros(kcs_ref.shape, F32)
    vcs_ref[...] = jnp.zeros(vcs_ref.shape, F32)
    per_page = PAGE_SIZE // NSA_BLOCK
    for src, dst, new in ((k_refs, kcs_ref, kn_ref), (v_refs, vcs_ref, vn_ref)):
        for p in range(n_pages):
            x = src[p][0, 0] * w2
            for h in range(per_page):
                dst[per_page * p + h:per_page * p + h + 1, :] = jnp.sum(
                    x[h * NSA_BLOCK:(h + 1) * NSA_BLOCK], axis=0, keepdims=True)
        blk_new = past // NSA_BLOCK
        dst[blk_new:blk_new + 1, :] = jnp.sum(new[0] * w2_ref[0:n_new, :], axis=0, keepdims=True)
    s = _dot_nt(qbd_ref[0], kcs_ref[...].astype(BF16))
    blk = lax.broadcasted_iota(jnp.int32, (rows, LANES), 1)
    qpos = past + (lax.broadcasted_iota(jnp.int32, (rows, LANES), 0) % (n_new * g)) // g
    pn = _masked_softmax(s, (blk + 1) * NSA_BLOCK - 1 <= qpos)
    oc_ref[0] = _fold_own_group(_dot(pn.astype(BF16), vcs_ref[...].astype(BF16)), g)
    qg = n_new * g
    pn_sum = pn[0:qg]
    for r in range(1, NSA_GROUP):
        pn_sum = pn_sum + pn[r * qg:(r + 1) * qg]
    blk_q = lax.broadcasted_iota(jnp.int32, (qg, LANES), 1)
    qpos_q = past + lax.broadcasted_iota(jnp.int32, (qg, LANES), 0) // g
    score = _block_scores(pn_sum, blk_q, qpos_q)
    sel = _top_n_mask(score, blk_q.astype(F32), NSA_TOP_N)
    sel_ref[0] = _tile_rows(sel, NSA_GROUP).astype(sel_ref.dtype)


def _nsa_dec_attn_kernel(pt_ref, qbd_ref, sel_ref, e_ref, ksn_ref, vsn_ref, kwn_ref, vwn_ref,
                         wk_ref, wv_ref, *rest, n_pages, n_new, past):
    k_refs, v_refs = rest[:n_pages], rest[n_pages:2 * n_pages]
    os_ref, ow_ref = rest[2 * n_pages:]
    g = NSA_KV_HEADS
    qbd = qbd_ref[0]
    rows, w = qbd.shape
    qf = qbd.astype(F32)
    sel = sel_ref[0]
    rowq = (lax.broadcasted_iota(jnp.int32, (rows, 1), 0) % (n_new * g)) // g
    empty = (jnp.full((rows, 1), NEG_BIG, F32), jnp.zeros((rows, 1), F32), jnp.zeros((rows, w), F32))

    state = empty
    for p in range(n_pages):
        s = _dot_nt(qbd, k_refs[p][0, 0].astype(BF16))
        chosen = _dot(sel, e_ref[:, p * PAGE_SIZE:(p + 1) * PAGE_SIZE])
        state = _block_update(state, s + (chosen - 1.0) * (-NEG_BIG), v_refs[p][0, 0].astype(BF16))
    blk_new = past // NSA_BLOCK
    new_sel = sel[:, blk_new:blk_new + 1].astype(F32) > 0.5
    ksn = ksn_ref[0]
    cols = []
    for j in range(n_new):
        sj = jnp.sum(qf * ksn[j:j + 1, :], axis=-1, keepdims=True)
        cols.append(jnp.where((j <= rowq) & new_sel, sj, NEG_BIG))
    m, l, acc = _column_update(state, cols, vsn_ref[0])
    os_ref[0] = _fold_own_group(acc / l, g)

    kwn = kwn_ref[0]
    cols = []
    for j in range(n_new):
        sj = jnp.sum(qf * kwn[j:j + 1, :], axis=-1, keepdims=True)
        cols.append(jnp.where(j <= rowq, sj, NEG_BIG))
    state = _column_update(empty, cols, vwn_ref[0])
    wb = wk_ref.shape[2]
    s = _dot_nt(qbd, wk_ref[0, 0].astype(BF16))
    slot = lax.broadcasted_iota(jnp.int32, (rows, wb), 1)
    dist = wb + rowq - slot
    vis = (dist >= 0) & (dist < NSA_WINDOW) & (past - wb + slot >= 0)
    m, l, acc = _block_update(state, jnp.where(vis, s, NEG_BIG), wv_ref[0, 0].astype(BF16))
    ow_ref[0] = _fold_own_group(acc / l, g)


def nsa_decode(pr, caches, win_k, win_v, layer, page_table, w_cmp, past):
    cmp_k, cmp_v, slc_k, slc_v = caches
    bsz, n_pages = page_table.shape
    kvw = NSA_KV_HEADS * HEAD_DIM
    n_new = pr["kc"].shape[0] // bsz
    heads = pr["qraw"].shape[1] // HEAD_DIM
    rows = n_new * heads
    assert past % NSA_BLOCK == 0 and n_new <= NSA_BLOCK and past // NSA_BLOCK < LANES
    new = lambda name: pr[name].reshape(bsz, n_new, kvw)
    qbd = lambda name: _block_diag_queries(
        pr[name].reshape(bsz, n_new, heads, HEAD_DIM), NSA_KV_HEADS, "rqg")
    w2 = jnp.tile(jnp.repeat(w_cmp, HEAD_DIM, axis=1), (PAGE_SIZE // NSA_BLOCK, 1))
    page = lambda i: pl.BlockSpec((1, 1, PAGE_SIZE, kvw), lambda b, pt: (layer, pt[b * n_pages + i], 0, 0))
    per_b = lambda shape: pl.BlockSpec((1,) + shape, lambda b, pt: (b, 0, 0))
    const = lambda shape: pl.BlockSpec(shape, lambda b, pt: (0,) * len(shape))
    pt_flat = page_table.reshape(-1)
    sds = jax.ShapeDtypeStruct

    oc, sel = pl.pallas_call(
        functools.partial(_nsa_dec_cmp_kernel, n_pages=n_pages, n_new=n_new, past=past),
        grid_spec=pltpu.PrefetchScalarGridSpec(
            num_scalar_prefetch=1, grid=(bsz,),
            in_specs=[per_b((rows, kvw)), per_b((n_new, kvw)), per_b((n_new, kvw)), const(w2.shape)]
                     + [page(i) for i in range(n_pages)] * 2,
            out_specs=[per_b((rows, HEAD_DIM)), per_b((rows, LANES))],
            scratch_shapes=[pltpu.VMEM((LANES, kvw), F32), pltpu.VMEM((LANES, kvw), F32)]),
        out_shape=[sds((bsz, rows, HEAD_DIM), F32), sds((bsz, rows, LANES), BF16)],
        compiler_params=_params("parallel"),
        name="nsa_decode_cmp_select",
    )(pt_flat, qbd("qraw"), new("kc"), new("vc"), w2, *([cmp_k] * n_pages), *([cmp_v] * n_pages))

    n_tok = n_pages * PAGE_SIZE
    expand = (jnp.arange(LANES)[:, None] == jnp.arange(n_tok)[None, :] // NSA_BLOCK).astype(BF16)
    wb = win_k.shape[2]
    win = pl.BlockSpec((1, 1, wb, kvw), lambda b, pt: (layer, b, 0, 0))
    o_s, o_w = pl.pallas_call(
        functools.partial(_nsa_dec_attn_kernel, n_pages=n_pages, n_new=n_new, past=past),
        grid_spec=pltpu.PrefetchScalarGridSpec(
            num_scalar_prefetch=1, grid=(bsz,),
            in_specs=[per_b((rows, kvw)), per_b((rows, LANES)), const(expand.shape)]
                     + [per_b((n_new, kvw))] * 4 + [win, win]
                     + [page(i) for i in range(n_pages)] * 2,
            out_specs=[per_b((rows, HEAD_DIM))] * 2),
        out_shape=[sds((bsz, rows, HEAD_DIM), F32)] * 2,
        compiler_params=_params("parallel"),
        name="nsa_decode_attn",
    )(pt_flat, qbd("qrot"), sel, expand, new("ks"), new("vs"), new("kw"), new("vw"),
      win_k, win_v, *([slc_k] * n_pages), *([slc_v] * n_pages))

    def to_tokens(o):
        o = o.reshape(bsz, NSA_GROUP, n_new, NSA_KV_HEADS, HEAD_DIM)
        return o.transpose(0, 2, 3, 1, 4).reshape(bsz * n_new, heads * HEAD_DIM)

    return to_tokens(oc), to_tokens(o_s), to_tokens(o_w)


def _rope_tables(pos):
    half = HEAD_DIM // 2
    inv_freq = ROPE_THETA ** (-jnp.arange(half, dtype=F32) / half)
    ang = pos.astype(F32)[:, None] * inv_freq[None, :]
    cos, sin = jnp.cos(ang), jnp.sin(ang)
    reps = LANES // HEAD_DIM
    return (jnp.tile(cos, (1, 2 * reps)), jnp.tile(jnp.concatenate([-sin, sin], axis=1), (1, reps)))


def _expand_gates(gate, heads):
    m = gate.shape[0]
    g = jnp.repeat(gate.reshape(m, 3, heads), HEAD_DIM, axis=-1)
    return g[:, 0], g[:, 1], g[:, 2]


def kernel(x_prompt, x_sample, cache_fox_k, cache_fox_v, cache_fox_logf, cache_nsa_cmp_k, cache_nsa_cmp_v, cache_nsa_slc_k, cache_nsa_slc_v, state_nsa_win_k, state_nsa_win_v, page_table, fox_w_in, fox_b_f, fox_w_o, nsa_w_in, nsa_b_gate, nsa_w_cmp, nsa_w_o, ffn_w_gu, ffn_w_down, ln_mix_g, ln_mix_b, ln_ffn_g, ln_ffn_b):
    batch, seq, d = x_prompt.shape
    dec_b, dec_t, _ = x_sample.shape
    depth = ffn_w_gu.shape[0]
    heads = fox_b_f.shape[1]
    n_pages = page_table.shape[1]
    past = n_pages * PAGE_SIZE
    alpha = (2 * depth) ** 0.25
    n_phys = cache_fox_k.shape[1]
    kvw = NSA_KV_HEADS * HEAD_DIM

    yp = x_prompt.reshape(batch * seq, d)
    ys = x_sample.reshape(dec_b * dec_t, d)
    fox_k_pool = cache_fox_k.reshape(cache_fox_k.shape[0], n_phys, PAGE_SIZE, heads * HEAD_DIM)
    fox_v_pool = cache_fox_v.reshape(fox_k_pool.shape)
    nsa_pools = [c.reshape(c.shape[0], n_phys, PAGE_SIZE, kvw)
                 for c in (cache_nsa_cmp_k, cache_nsa_cmp_v, cache_nsa_slc_k, cache_nsa_slc_v)]
    wb = state_nsa_win_k.shape[2]
    win_k = state_nsa_win_k.reshape(state_nsa_win_k.shape[0], dec_b, wb, kvw)
    win_v = state_nsa_win_v.reshape(win_k.shape)
    cos_p, sin_p = _rope_tables(jnp.tile(jnp.arange(seq), batch))
    cos_s, sin_s = _rope_tables(jnp.tile(past + jnp.arange(dec_t), dec_b))

    fox_new_p, fox_new_s, nsa_new_p, nsa_new_s = [], [], [], []
    for i in range(depth):
        j = i // 2
        if i % 2 == 0:
            w_in, b_f, w_o = fox_w_in[j], fox_b_f[j], fox_w_o[j]
            q, k, v, kb, vb, lf = fox_project(yp, w_in, b_f)
            lf3 = lf.reshape(batch, seq, heads)
            o = fox_prompt_attention(q, kb, vb, jnp.cumsum(lf3, axis=1), batch)
            fox_new_p.append((k.reshape(batch, seq, heads, HEAD_DIM), v.reshape(batch, seq, heads, HEAD_DIM), lf3))
            yp = wo_post_norm(yp, [o], w_o, ln_mix_g[i], ln_mix_b[i], alpha)

            q, k, v, _, _, lf = fox_project(ys, w_in, b_f)
            lf3 = lf.reshape(dec_b, dec_t, heads)
            lf_past = cache_fox_logf[j][page_table].reshape(dec_b, past, heads).astype(F32)
            f_cum = jnp.swapaxes(jnp.cumsum(jnp.concatenate([lf_past, lf3], axis=1), axis=1), 1, 2)
            o = fox_decode_attention(q.reshape(dec_b, dec_t, -1), k.reshape(dec_b, dec_t, -1),
                                     v.reshape(dec_b, dec_t, -1), f_cum[:, :, :past], f_cum[:, :, past:],
                                     fox_k_pool, fox_v_pool, j, page_table)
            fox_new_s.append((k.reshape(dec_b, dec_t, heads, HEAD_DIM), v.reshape(dec_b, dec_t, heads, HEAD_DIM), lf3))
            ys = wo_post_norm(ys, [o.reshape(dec_b * dec_t, -1)], w_o, ln_mix_g[i], ln_mix_b[i], alpha)
        else:
            w_in, b_g, w_c, w_o = nsa_w_in[j], nsa_b_gate[j], nsa_w_cmp[j], nsa_w_o[j]
            pr = nsa_project(yp, w_in, b_g, w_c, cos_p, sin_p)
            oc, sel = nsa_prompt_compress_select(pr["qraw"], pr["kcsd"], pr["vcsd"], batch)
            o_s = nsa_prompt_selected(pr["qrot"], pr["ksd"], pr["vsd"], sel, batch)
            o_w = nsa_prompt_window(pr["qrot"], pr["kwd"], pr["vwd"], batch)
            shp = (batch, seq, NSA_KV_HEADS, HEAD_DIM)
            keep = min(NSA_WINDOW, seq)
            nsa_new_p.append(tuple(pr[n].reshape(shp) for n in ("kc", "vc", "ks", "vs"))
                             + tuple(pr[n].reshape(shp)[:, seq - keep:] for n in ("kw", "vw")))
            yp = wo_post_norm(yp, [oc, o_s, o_w, *_expand_gates(pr["gate"], heads)], w_o,
                              ln_mix_g[i], ln_mix_b[i], alpha)

            pr = nsa_project(ys, w_in, b_g, w_c, cos_s, sin_s)
            oc, o_s, o_w = nsa_decode(pr, nsa_pools, win_k, win_v, j, page_table, w_c, past)
            shp = (dec_b, dec_t, NSA_KV_HEADS, HEAD_DIM)
            keep = min(NSA_WINDOW, wb + dec_t)
            kw_all = jnp.concatenate([state_nsa_win_k[j], pr["kw"].reshape(shp)], axis=1)
            vw_all = jnp.concatenate([state_nsa_win_v[j], pr["vw"].reshape(shp)], axis=1)
            nsa_new_s.append(tuple(pr[n].reshape(shp) for n in ("kc", "vc", "ks", "vs"))
                             + (kw_all[:, wb + dec_t - keep:], vw_all[:, wb + dec_t - keep:]))
            ys = wo_post_norm(ys, [oc, o_s, o_w, *_expand_gates(pr["gate"], heads)], w_o,
                              ln_mix_g[i], ln_mix_b[i], alpha)
        yp = ffn_post_norm(yp, ffn_w_gu[i], ffn_w_down[i], ln_ffn_g[i], ln_ffn_b[i], alpha)
        ys = ffn_post_norm(ys, ffn_w_gu[i], ffn_w_down[i], ln_ffn_g[i], ln_ffn_b[i], alpha)

    fp = [jnp.stack(a) for a in zip(*fox_new_p)]
    fs = [jnp.stack(a) for a in zip(*fox_new_s)]
    sp = [jnp.stack(a) for a in zip(*nsa_new_p)]
    ss = [jnp.stack(a) for a in zip(*nsa_new_s)]
    return (yp.reshape(batch, seq, d), ys.reshape(dec_b, dec_t, d), fp[0], fs[0], fp[1], fs[1], fp[2], fs[2],
            sp[0], ss[0], sp[1], ss[1], sp[2], ss[2], sp[3], ss[3], sp[4], ss[4], sp[5], ss[5])
```

```python
import functools

import jax
import jax.numpy as jnp
from jax import lax
from jax.experimental import pallas as pl
from jax.experimental.pallas import tpu as pltpu

F32 = jnp.float32
BF16 = jnp.bfloat16

LANES = 128
HEAD_DIM = 64
NSA_KV_HEADS = 4
NSA_GROUP = 4
NSA_BLOCK = 64
NSA_TOP_N = 16
NSA_WINDOW = 512
PAGE_SIZE = 128
ROPE_THETA = 10000.0
LN_EPS = 1e-5
NEG_BIG = -1e30
SEL_FORCE = 1e9
VMEM_LIMIT_BYTES = 56 * 1024 * 1024

TOKEN_TILE = 512
FOX_TILE = 512
NSA_Q_TILE = 256
NSA_K_TILE = 512
NSA_CMP_TILE = 512
DEC_PAGES_PER_STEP = 8


def _dot(a, b):
    return jnp.dot(a, b, preferred_element_type=F32)


def _dot_nt(a, b):
    return lax.dot_general(a, b, (((1,), (1,)), ((), ())), preferred_element_type=F32)


def _params(*sem):
    return pltpu.CompilerParams(dimension_semantics=sem, vmem_limit_bytes=VMEM_LIMIT_BYTES)


def _layer_norm(x, g, b):
    mu = jnp.mean(x, axis=-1, keepdims=True)
    xc = x - mu
    var = jnp.mean(xc * xc, axis=-1, keepdims=True)
    return xc * lax.rsqrt(var + LN_EPS) * g + b


def _full(shape):
    n = len(shape)
    return pl.BlockSpec(shape, lambda *_: (0,) * n)


def _fox_proj_kernel(x_ref, w_ref, wf_ref, bf_ref, q_ref, k_ref, v_ref, kb_ref, vb_ref, lf_ref):
    aw = q_ref.shape[1]
    x = x_ref[...].astype(BF16)
    q_ref[...] = _dot(x, w_ref[:, 0:aw]).astype(BF16)
    k = _dot(x, w_ref[:, aw:2 * aw])
    k_ref[...] = k
    kb_ref[...] = k.astype(BF16)
    v = _dot(x, w_ref[:, 2 * aw:3 * aw])
    v_ref[...] = v
    vb_ref[...] = v.astype(BF16)
    z = _dot(x, wf_ref[...]) + bf_ref[...]
    lf = jnp.minimum(z, 0.0) - jnp.log1p(jnp.exp(-jnp.abs(z)))
    lf_ref[...] = lf[:, :lf_ref.shape[1]]


def fox_project(x, w_in, b_f):
    m, d = x.shape
    nh = b_f.shape[0]
    aw = (w_in.shape[1] - nh) // 3
    scale = HEAD_DIM ** -0.5
    w_qkv = jnp.concatenate([w_in[:, :aw] * scale, w_in[:, aw:3 * aw]], axis=1).astype(BF16)
    w_f = jnp.pad(w_in[:, 3 * aw:], ((0, 0), (0, LANES - nh))).astype(BF16)
    b = jnp.pad(b_f, (0, LANES - nh)).reshape(1, LANES)
    tm = min(TOKEN_TILE, m)
    row = lambda w: pl.BlockSpec((tm, w), lambda i: (i, 0))
    return pl.pallas_call(
        _fox_proj_kernel,
        grid=(m // tm,),
        in_specs=[row(d), _full(w_qkv.shape), _full(w_f.shape), _full(b.shape)],
        out_specs=[row(aw), row(aw), row(aw), row(aw), row(aw), row(nh)],
        out_shape=[jax.ShapeDtypeStruct((m, aw), BF16), jax.ShapeDtypeStruct((m, aw), F32),
                   jax.ShapeDtypeStruct((m, aw), F32), jax.ShapeDtypeStruct((m, aw), BF16),
                   jax.ShapeDtypeStruct((m, aw), BF16), jax.ShapeDtypeStruct((m, nh), F32)],
        compiler_params=_params("parallel"),
        name="fox_proj",
    )(x, w_qkv, w_f, b)


def _rope(x, cos, sin_signed, lo_half):
    outs = []
    for c in range(x.shape[1] // LANES):
        xc = x[:, c * LANES:(c + 1) * LANES]
        swapped = jnp.where(lo_half, pltpu.roll(xc, LANES - HEAD_DIM // 2, 1),
                            pltpu.roll(xc, HEAD_DIM // 2, 1))
        outs.append(xc * cos + swapped * sin_signed)
    return outs


def _nsa_proj_kernel(x_ref, w_ref, bg_ref, wc_ref, cos_ref, sin_ref,
                     qraw_ref, qrot_ref, kc_ref, vc_ref, ks_ref, vs_ref, kw_ref, vw_ref,
                     ksd_ref, vsd_ref, kwd_ref, vwd_ref, kcsd_ref, vcsd_ref, gate_ref):
    aw = qraw_ref.shape[1]
    kvw = kc_ref.shape[1]
    dw = ksd_ref.shape[1]
    tm = x_ref.shape[0]
    x = x_ref[...].astype(BF16)
    cos = cos_ref[...]
    sin = sin_ref[...]
    lane = lax.broadcasted_iota(jnp.int32, (tm, LANES), 1)
    lo_half = (lane % HEAD_DIM) < (HEAD_DIM // 2)

    def rope_store(val, refs):
        for c, piece in enumerate(_rope(val, cos, sin, lo_half)):
            for r in refs:
                r[:, c * LANES:(c + 1) * LANES] = piece.astype(r.dtype)

    q = _dot(x, w_ref[:, 0:aw])
    qraw_ref[...] = q.astype(BF16)
    rope_store(q, [qrot_ref])
    off = aw
    wc = wc_ref[...]
    nblk = tm // NSA_BLOCK

    def summary(val, out_ref):
        s = jnp.sum(val.reshape(nblk, NSA_BLOCK, kvw) * wc[None], axis=1)
        for g in range(NSA_KV_HEADS):
            sg = s[:, g * HEAD_DIM:(g + 1) * HEAD_DIM].astype(out_ref.dtype)
            out_ref[:, g * LANES:g * LANES + HEAD_DIM] = sg
            out_ref[:, g * LANES + HEAD_DIM:(g + 1) * LANES] = sg

    kc = _dot(x, w_ref[:, off:off + kvw])
    kc_ref[...] = kc
    summary(kc, kcsd_ref)
    vc = _dot(x, w_ref[:, off + kvw:off + 2 * kvw])
    vc_ref[...] = vc
    summary(vc, vcsd_ref)
    rope_store(_dot(x, w_ref[:, off + 2 * kvw:off + 3 * kvw]), [ks_ref])
    vs_ref[...] = _dot(x, w_ref[:, off + 3 * kvw:off + 4 * kvw])
    rope_store(_dot(x, w_ref[:, off + 4 * kvw:off + 5 * kvw]), [kw_ref])
    vw_ref[...] = _dot(x, w_ref[:, off + 5 * kvw:off + 6 * kvw])
    off += 6 * kvw
    rope_store(_dot(x, w_ref[:, off:off + dw]), [ksd_ref])
    vsd_ref[...] = _dot(x, w_ref[:, off + dw:off + 2 * dw]).astype(BF16)
    rope_store(_dot(x, w_ref[:, off + 2 * dw:off + 3 * dw]), [kwd_ref])
    vwd_ref[...] = _dot(x, w_ref[:, off + 3 * dw:off + 4 * dw]).astype(BF16)
    off += 4 * dw
    z = _dot(x, w_ref[:, off:off + LANES]) + bg_ref[...]
    gate = 1.0 / (1.0 + jnp.exp(-z))
    gate_ref[...] = gate[:, :gate_ref.shape[1]]


def _dup_heads(w):
    d, n = w.shape
    g = n // HEAD_DIM
    w = w.reshape(d, g, 1, HEAD_DIM)
    return jnp.broadcast_to(w, (d, g, 2, HEAD_DIM)).reshape(d, g * LANES)


def nsa_project(x, w_in, b_gate, w_cmp, cos_t, sin_t):
    m, d = x.shape
    ng = b_gate.shape[0]
    aw = (ng // 3) * HEAD_DIM
    kvw = NSA_KV_HEADS * HEAD_DIM
    dw = NSA_KV_HEADS * LANES
    scale = HEAD_DIM ** -0.5
    kv = w_in[:, aw:aw + 6 * kvw]
    cols = [w_in[:, :aw] * scale, kv]
    for i in (2, 3, 4, 5):
        cols.append(_dup_heads(kv[:, i * kvw:(i + 1) * kvw]))
    cols.append(jnp.pad(w_in[:, aw + 6 * kvw:], ((0, 0), (0, LANES - ng))))
    w_all = jnp.concatenate(cols, axis=1).astype(BF16)
    bg = jnp.pad(b_gate, (0, LANES - ng)).reshape(1, LANES)
    wc = jnp.repeat(w_cmp, HEAD_DIM, axis=1)
    tm = min(TOKEN_TILE, m)
    row = lambda w: pl.BlockSpec((tm, w), lambda i: (i, 0))
    sds = jax.ShapeDtypeStruct
    nblk = tm // NSA_BLOCK
    outs = pl.pallas_call(
        _nsa_proj_kernel,
        grid=(m // tm,),
        in_specs=[row(d), _full(w_all.shape), _full(bg.shape), _full(wc.shape), row(LANES), row(LANES)],
        out_specs=[row(aw), row(aw)] + [row(kvw)] * 6 + [row(dw)] * 4
                  + [pl.BlockSpec((nblk, dw), lambda i: (i, 0))] * 2 + [row(ng)],
        out_shape=[sds((m, aw), BF16), sds((m, aw), BF16)] + [sds((m, kvw), F32)] * 6
                  + [sds((m, dw), BF16)] * 4 + [sds((m // NSA_BLOCK, dw), BF16)] * 2 + [sds((m, ng), F32)],
        compiler_params=_params("parallel"),
        name="nsa_proj",
    )(x, w_all, bg, wc, cos_t, sin_t)
    names = ("qraw", "qrot", "kc", "vc", "ks", "vs", "kw", "vw", "ksd", "vsd", "kwd", "vwd", "kcsd", "vcsd", "gate")
    return dict(zip(names, outs))


def _wo_ln_kernel(x_ref, *rest, gated, alpha):
    if gated:
        oc, os_, ow, gc, gs, gw, w_ref, g_ref, b_ref, y_ref = rest
        o = gc[...] * oc[...] + gs[...] * os_[...] + gw[...] * ow[...]
    else:
        o_ref, w_ref, g_ref, b_ref, y_ref = rest
        o = o_ref[...]
    h = _dot(o.astype(BF16), w_ref[...])
    y_ref[...] = _layer_norm(alpha * x_ref[...] + h, g_ref[...], b_ref[...])


def wo_post_norm(x, branches, w_o, g, b, alpha):
    m, d = x.shape
    aw = w_o.shape[0]
    tm = min(TOKEN_TILE, m)
    row = lambda w: pl.BlockSpec((tm, w), lambda i: (i, 0))
    return pl.pallas_call(
        functools.partial(_wo_ln_kernel, gated=len(branches) > 1, alpha=alpha),
        grid=(m // tm,),
        in_specs=[row(d)] + [row(aw)] * len(branches) + [_full(w_o.shape), _full((1, d)), _full((1, d))],
        out_specs=row(d),
        out_shape=jax.ShapeDtypeStruct((m, d), F32),
        compiler_params=_params("parallel"),
        name="wo_post_norm",
    )(x, *branches, w_o.astype(BF16), g.reshape(1, d), b.reshape(1, d))


def _ffn_kernel(x_ref, wg_ref, wu_ref, wd_ref, g_ref, b_ref, y_ref, acc_ref, *, alpha):
    f = pl.program_id(1)
    x = x_ref[...]
    xb = x.astype(BF16)
    gate = _dot(xb, wg_ref[...])
    up = _dot(xb, wu_ref[...])
    h = gate / (1.0 + jnp.exp(-gate)) * up
    part = _dot(h.astype(BF16), wd_ref[...])

    @pl.when(f == 0)
    def _():
        acc_ref[...] = part

    @pl.when(f > 0)
    def _():
        acc_ref[...] += part

    @pl.when(f == pl.num_programs(1) - 1)
    def _():
        y_ref[...] = _layer_norm(alpha * x + acc_ref[...], g_ref[...], b_ref[...])


def ffn_post_norm(x, w_gu, w_down, g, b, alpha):
    m, d = x.shape
    dff = w_down.shape[0]
    nf = 2
    tf = dff // nf
    tm = min(TOKEN_TILE, m)
    w_gu = w_gu.astype(BF16)
    return pl.pallas_call(
        functools.partial(_ffn_kernel, alpha=alpha),
        grid=(m // tm, nf),
        in_specs=[pl.BlockSpec((tm, d), lambda i, f: (i, 0)),
                  pl.BlockSpec((d, tf), lambda i, f: (0, f)),
                  pl.BlockSpec((d, tf), lambda i, f: (0, nf + f)),
                  pl.BlockSpec((tf, d), lambda i, f: (f, 0)),
                  pl.BlockSpec((1, d), lambda i, f: (0, 0)),
                  pl.BlockSpec((1, d), lambda i, f: (0, 0))],
        out_specs=pl.BlockSpec((tm, d), lambda i, f: (i, 0)),
        out_shape=jax.ShapeDtypeStruct((m, d), F32),
        scratch_shapes=[pltpu.VMEM((tm, d), F32)],
        compiler_params=_params("parallel", "arbitrary"),
        name="ffn_post_norm",
    )(x, w_gu, w_gu, w_down.astype(BF16), g.reshape(1, d), b.reshape(1, d))


def _stack_heads(q_ref, qs_ref, tq):
    lane = lax.broadcasted_iota(jnp.int32, (tq, LANES), 1)
    lo = lane < HEAD_DIM
    for c in range(q_ref.shape[1] // LANES):
        qc = q_ref[:, c * LANES:(c + 1) * LANES].astype(F32)
        qs_ref[(2 * c) * tq:(2 * c + 1) * tq, :] = jnp.where(lo, qc, 0.0).astype(qs_ref.dtype)
        qs_ref[(2 * c + 1) * tq:(2 * c + 2) * tq, :] = jnp.where(lo, 0.0, qc).astype(qs_ref.dtype)


def _unstack_heads(vals, o_ref):
    tq = vals.shape[1]
    lane = lax.broadcasted_iota(jnp.int32, (tq, LANES), 1)
    lo = lane < HEAD_DIM
    for c in range(vals.shape[0] // 2):
        o_ref[:, c * LANES:(c + 1) * LANES] = jnp.where(lo, vals[2 * c], vals[2 * c + 1])


def _softmax_init(m_ref, l_ref, acc_ref):
    m_ref[...] = jnp.full(m_ref.shape, NEG_BIG, F32)
    l_ref[...] = jnp.zeros(l_ref.shape, F32)
    acc_ref[...] = jnp.zeros(acc_ref.shape, F32)


def _softmax_update(s, v_tile, m_ref, l_ref, acc_ref):
    tk = s.shape[1]
    m_prev = m_ref[...]
    m_new = jnp.maximum(m_prev, jnp.max(s, axis=-1, keepdims=True))
    alpha = jnp.exp(m_prev - m_new)
    p = jnp.exp(s - jnp.tile(m_new, (1, tk // LANES)))
    l_ref[...] = alpha * l_ref[...] + jnp.sum(p, axis=-1, keepdims=True)
    acc_ref[...] = alpha * acc_ref[...] + _dot(p.astype(BF16), v_tile)
    m_ref[...] = m_new


def _attn_scratch(rows):
    return [pltpu.VMEM((rows, LANES), BF16), pltpu.VMEM((rows, LANES), F32),
            pltpu.VMEM((rows, LANES), F32), pltpu.VMEM((rows, LANES), F32)]


def _fox_attn_kernel(q_ref, k_ref, v_ref, f_ref, o_ref, qs_ref, m_ref, l_ref, acc_ref, *, t):
    qi = pl.program_id(2)
    _stack_heads(q_ref, qs_ref, t)
    _softmax_init(m_ref, l_ref, acc_ref)
    q0 = pl.multiple_of(qi * t, t)
    f_q0 = f_ref[0, 0, :, pl.ds(q0, LANES)][:, 0:1]

    def tile(kt, causal):
        k0 = pl.multiple_of(kt * t, t)
        s = _dot_nt(qs_ref[...], k_ref[pl.ds(k0, t), :])
        bias = f_q0 - f_ref[0, 0, :, pl.ds(k0, t)]
        s3 = s.reshape(2, t, t) + bias[:, None, :]
        if causal:
            row = lax.broadcasted_iota(jnp.int32, (t, t), 0)
            col = lax.broadcasted_iota(jnp.int32, (t, t), 1)
            s3 = jnp.where((col <= row)[None], s3, NEG_BIG)
        _softmax_update(s3.reshape(2 * t, t), v_ref[pl.ds(k0, t), :], m_ref, l_ref, acc_ref)

    def body(kt, carry):
        tile(kt, False)
        return carry

    lax.fori_loop(0, qi, body, 0)
    tile(qi, True)
    _unstack_heads((acc_ref[...] / l_ref[...]).reshape(2, t, LANES), o_ref)


def fox_prompt_attention(q, k, v, f_cum, batch):
    m, aw = q.shape
    seq = m // batch
    t = FOX_TILE
    nq = seq // t
    npair = aw // LANES
    f = jnp.swapaxes(f_cum, 1, 2).reshape(batch, npair, 2, seq)
    return pl.pallas_call(
        functools.partial(_fox_attn_kernel, t=t),
        grid=(batch, npair, nq),
        in_specs=[pl.BlockSpec((t, LANES), lambda b, c, i: (b * nq + i, c)),
                  pl.BlockSpec((seq, LANES), lambda b, c, i: (b, c)),
                  pl.BlockSpec((seq, LANES), lambda b, c, i: (b, c)),
                  pl.BlockSpec((1, 1, 2, seq), lambda b, c, i: (b, c, 0, 0))],
        out_specs=pl.BlockSpec((t, LANES), lambda b, c, i: (b * nq + i, c)),
        out_shape=jax.ShapeDtypeStruct((m, aw), F32),
        scratch_shapes=_attn_scratch(2 * t),
        compiler_params=_params("parallel", "parallel", "arbitrary"),
        name="fox_prompt_attn",
    )(q, k, v, f)


def _top_n_mask(score, blk_f, n_sel):
    sel = jnp.zeros(score.shape, F32)
    for _ in range(n_sel):
        mx = jnp.max(score, axis=-1, keepdims=True)
        idx = jnp.min(jnp.where(score == mx, blk_f, SEL_FORCE), axis=-1, keepdims=True)
        hit = blk_f == idx
        sel = jnp.where(hit, 1.0, sel)
        score = jnp.where(hit, -jnp.inf, score)
    return sel


def _block_scores(pn_sum, blk, qpos):
    cur = qpos // NSA_BLOCK
    forced = (blk == 0) | (blk == cur) | (blk == cur - 1)
    return jnp.where(blk <= cur, jnp.where(forced, SEL_FORCE, pn_sum), -SEL_FORCE)


def _masked_softmax(s, vis):
    s = jnp.where(vis, s, NEG_BIG)
    p = jnp.where(vis, jnp.exp(s - jnp.max(s, axis=-1, keepdims=True)), 0.0)
    return p / jnp.maximum(jnp.sum(p, axis=-1, keepdims=True), 1e-30)


def _nsa_cmp_kernel(q_ref, kk_ref, vv_ref, oc_ref, sel_ref, qs_ref, *, tq):
    qi = pl.program_id(2)
    nb = kk_ref.shape[0]
    _stack_heads(q_ref, qs_ref, tq)
    s3 = _dot_nt(qs_ref[...], kk_ref[...]).reshape(NSA_GROUP, tq, nb)
    blk = lax.broadcasted_iota(jnp.int32, (tq, nb), 1)
    qpos = qi * tq + lax.broadcasted_iota(jnp.int32, (tq, nb), 0)
    vis = (blk + 1) * NSA_BLOCK - 1 <= qpos
    pn = _masked_softmax(s3, vis[None])
    oc = _dot(pn.reshape(NSA_GROUP * tq, nb).astype(BF16), vv_ref[...])
    _unstack_heads(oc.reshape(NSA_GROUP, tq, LANES), oc_ref)
    score = _block_scores(jnp.sum(pn, axis=0), blk, qpos)
    sel_ref[0, 0] = _top_n_mask(score, blk.astype(F32), min(NSA_TOP_N, nb)).astype(sel_ref.dtype)


def nsa_prompt_compress_select(qraw, kcsd, vcsd, batch):
    m, aw = qraw.shape
    seq = m // batch
    nb = seq // NSA_BLOCK
    assert nb <= LANES, "block axis is mapped onto one vreg lane row"
    tq = NSA_CMP_TILE
    nq = seq // tq
    gw = NSA_GROUP * HEAD_DIM
    ng = aw // gw
    return pl.pallas_call(
        functools.partial(_nsa_cmp_kernel, tq=tq),
        grid=(batch, ng, nq),
        in_specs=[pl.BlockSpec((tq, gw), lambda b, g, i: (b * nq + i, g)),
                  pl.BlockSpec((nb, LANES), lambda b, g, i: (b, g)),
                  pl.BlockSpec((nb, LANES), lambda b, g, i: (b, g))],
        out_specs=[pl.BlockSpec((tq, gw), lambda b, g, i: (b * nq + i, g)),
                   pl.BlockSpec((1, 1, tq, nb), lambda b, g, i: (b, g, i, 0))],
        out_shape=[jax.ShapeDtypeStruct((m, aw), F32), jax.ShapeDtypeStruct((batch, ng, seq, nb), BF16)],
        scratch_shapes=[pltpu.VMEM((NSA_GROUP * tq, LANES), BF16)],
        compiler_params=_params("parallel", "parallel", "parallel"),
        name="nsa_prompt_cmp_select",
    )(qraw, kcsd, vcsd)


def _nsa_sel_kernel(q_ref, k_ref, v_ref, sel_ref, e_ref, o_ref, qs_ref, m_ref, l_ref, acc_ref, *, tq, tk):
    qi = pl.program_id(2)
    _stack_heads(q_ref, qs_ref, tq)
    _softmax_init(m_ref, l_ref, acc_ref)
    sel = sel_ref[0, 0]
    q0 = qi * tq
    kt_diag = q0 // tk

    def tile(kt, causal):
        k0 = pl.multiple_of(kt * tk, tk)
        s = _dot_nt(qs_ref[...], k_ref[pl.ds(k0, tk), :])
        chosen = _dot(sel, e_ref[:, pl.ds(k0, tk)])
        s3 = s.reshape(NSA_GROUP, tq, tk) + ((chosen - 1.0) * (-NEG_BIG))[None]
        if causal:
            qpos = q0 + lax.broadcasted_iota(jnp.int32, (tq, tk), 0)
            kpos = k0 + lax.broadcasted_iota(jnp.int32, (tq, tk), 1)
            s3 = jnp.where((kpos <= qpos)[None], s3, NEG_BIG)
        _softmax_update(s3.reshape(NSA_GROUP * tq, tk), v_ref[pl.ds(k0, tk), :], m_ref, l_ref, acc_ref)

    def body(kt, carry):
        tile(kt, False)
        return carry

    lax.fori_loop(0, kt_diag, body, 0)
    tile(kt_diag, True)
    _unstack_heads((acc_ref[...] / l_ref[...]).reshape(NSA_GROUP, tq, LANES), o_ref)


def nsa_prompt_selected(qrot, ksd, vsd, sel, batch):
    m, aw = qrot.shape
    seq = m // batch
    nb = seq // NSA_BLOCK
    tq, tk = NSA_Q_TILE, NSA_K_TILE
    nq = seq // tq
    gw = NSA_GROUP * HEAD_DIM
    ng = aw // gw
    expand = (jnp.arange(nb)[:, None] == jnp.arange(seq)[None, :] // NSA_BLOCK).astype(BF16)
    return pl.pallas_call(
        functools.partial(_nsa_sel_kernel, tq=tq, tk=tk),
        grid=(batch, ng, nq),
        in_specs=[pl.BlockSpec((tq, gw), lambda b, g, i: (b * nq + i, g)),
                  pl.BlockSpec((seq, LANES), lambda b, g, i: (b, g)),
                  pl.BlockSpec((seq, LANES), lambda b, g, i: (b, g)),
                  pl.BlockSpec((1, 1, tq, nb), lambda b, g, i: (b, g, i, 0)),
                  _full((nb, seq))],
        out_specs=pl.BlockSpec((tq, gw), lambda b, g, i: (b * nq + i, g)),
        out_shape=jax.ShapeDtypeStruct((m, aw), F32),
        scratch_shapes=_attn_scratch(NSA_GROUP * tq),
        compiler_params=_params("parallel", "parallel", "arbitrary"),
        name="nsa_prompt_selected",
    )(qrot, ksd, vsd, sel, expand)


def _nsa_win_kernel(q_ref, k_ref, v_ref, o_ref, qs_ref, *, tq):
    qi = pl.program_id(2)
    span = NSA_WINDOW + tq
    _stack_heads(q_ref, qs_ref, tq)
    k0 = pl.multiple_of(jnp.maximum(qi * tq - NSA_WINDOW, 0), tq)
    s = _dot_nt(qs_ref[...], k_ref[pl.ds(k0, span), :])
    dist = (qi * tq - k0 + lax.broadcasted_iota(jnp.int32, (tq, span), 0)
            - lax.broadcasted_iota(jnp.int32, (tq, span), 1))
    vis = (dist >= 0) & (dist < NSA_WINDOW)
    s = jnp.where(vis[None], s.reshape(NSA_GROUP, tq, span), NEG_BIG).reshape(NSA_GROUP * tq, span)
    p = jnp.exp(s - jnp.max(s, axis=-1, keepdims=True))
    o = _dot(p.astype(BF16), v_ref[pl.ds(k0, span), :]) / jnp.sum(p, axis=-1, keepdims=True)
    _unstack_heads(o.reshape(NSA_GROUP, tq, LANES), o_ref)


def nsa_prompt_window(qrot, kwd, vwd, batch):
    m, aw = qrot.shape
    seq = m // batch
    tq = NSA_Q_TILE
    assert NSA_WINDOW % tq == 0 and seq >= NSA_WINDOW + tq
    nq = seq // tq
    gw = NSA_GROUP * HEAD_DIM
    ng = aw // gw
    return pl.pallas_call(
        functools.partial(_nsa_win_kernel, tq=tq),
        grid=(batch, ng, nq),
        in_specs=[pl.BlockSpec((tq, gw), lambda b, g, i: (b * nq + i, g)),
                  pl.BlockSpec((seq, LANES), lambda b, g, i: (b, g)),
                  pl.BlockSpec((seq, LANES), lambda b, g, i: (b, g))],
        out_specs=pl.BlockSpec((tq, gw), lambda b, g, i: (b * nq + i, g)),
        out_shape=jax.ShapeDtypeStruct((m, aw), F32),
        scratch_shapes=[pltpu.VMEM((NSA_GROUP * tq, LANES), BF16)],
        compiler_params=_params("parallel", "parallel", "parallel"),
        name="nsa_prompt_window",
    )(qrot, kwd, vwd)


def _tile_rows(x, n):
    return jnp.concatenate([x] * n, axis=0)


def _column_update(state, cols, v_rows):
    m_prev, l_prev, acc = state
    m_new = m_prev
    for c in cols:
        m_new = jnp.maximum(m_new, c)
    alpha = jnp.exp(m_prev - m_new)
    l_new = alpha * l_prev
    acc = alpha * acc
    for j, c in enumerate(cols):
        p = jnp.exp(c - m_new)
        l_new = l_new + p
        acc = acc + p * v_rows[j:j + 1, :]
    return m_new, l_new, acc


def _block_update(state, s, v_t):
    m_prev, l_prev, acc = state
    m_new = jnp.maximum(m_prev, jnp.max(s, axis=-1, keepdims=True))
    alpha = jnp.exp(m_prev - m_new)
    p = jnp.exp(s - m_new)
    l_new = alpha * l_prev + jnp.sum(p, axis=-1, keepdims=True)
    acc = alpha * acc + _dot_nt(p.astype(BF16), v_t)
    return m_new, l_new, acc


def _fox_dec_kernel(pt_ref, qbd_ref, kn_ref, vn_ref, fp_ref, fn_ref, *rest, pp, n_new, heads):
    k_refs, v_refs = rest[:pp], rest[pp:2 * pp]
    o_ref, m_ref, l_ref, acc_ref = rest[2 * pp:]
    step = pl.program_id(1)
    rows = n_new * heads
    qbd = qbd_ref[0]
    fn = fn_ref[0]
    f_ref0 = fn[:, 0:1]

    @pl.when(step == 0)
    def _():
        qf = qbd.astype(F32)
        kn = kn_ref[0]
        rowq = lax.broadcasted_iota(jnp.int32, (rows, 1), 0) // heads
        cols = []
        for j in range(n_new):
            sj = jnp.sum(qf * kn[j:j + 1, :], axis=-1, keepdims=True)
            sj = sj + _tile_rows(f_ref0 - fn[:, j:j + 1], n_new)
            cols.append(jnp.where(j <= rowq, sj, NEG_BIG))
        init = (jnp.full((rows, 1), NEG_BIG, F32), jnp.zeros((rows, 1), F32),
                jnp.zeros(acc_ref.shape, F32))
        m, l, acc = _column_update(init, cols, vn_ref[0])
        m_ref[...] = m
        l_ref[...] = l
        acc_ref[...] = acc

    state = (m_ref[...], l_ref[...], acc_ref[...])
    for i in range(pp):
        s = _dot(qbd, k_refs[i][0, 0].astype(BF16))
        bias = f_ref0 - fp_ref[0][:, i * PAGE_SIZE:(i + 1) * PAGE_SIZE]
        state = _block_update(state, s + _tile_rows(bias, n_new), v_refs[i][0, 0].astype(BF16))
    m_ref[...], l_ref[...], acc_ref[...] = state

    @pl.when(step == pl.num_programs(1) - 1)
    def _():
        w = acc_ref.shape[1]
        o = acc_ref[...] / l_ref[...]
        own = (lax.broadcasted_iota(jnp.int32, (rows, w), 0) % heads
               == lax.broadcasted_iota(jnp.int32, (rows, w), 1) // HEAD_DIM)
        o_ref[0] = jnp.sum(jnp.where(own, o, 0.0).reshape(n_new, heads, w), axis=1)


def _block_diag_queries(q, n_lane_groups, order):
    bsz, tq, h, dh = q.shape
    r = h // n_lane_groups
    onehot = (jnp.arange(h)[:, None] // r == jnp.arange(n_lane_groups)[None, :]).astype(q.dtype)
    x = q[:, :, :, None, :] * onehot[None, None, :, :, None]
    if order == "rqg":
        x = x.reshape(bsz, tq, n_lane_groups, r, n_lane_groups, dh).transpose(0, 3, 1, 2, 4, 5)
    return x.reshape(bsz, tq * h, n_lane_groups * dh)


def fox_decode_attention(q, k_new, v_new, f_past, f_new, cache_k, cache_v, layer, page_table):
    bsz, n_new, aw = q.shape
    heads = aw // HEAD_DIM
    n_pages = page_table.shape[1]
    pp = min(DEC_PAGES_PER_STEP, n_pages)
    rows = n_new * heads
    qbd = _block_diag_queries(q.reshape(bsz, n_new, heads, HEAD_DIM), heads, "qh")
    fn = jnp.pad(f_new, ((0, 0), (0, 0), (0, LANES - n_new)))
    page = lambda i: pl.BlockSpec(
        (1, 1, aw, PAGE_SIZE), lambda b, s, pt: (layer, pt[b * n_pages + s * pp + i], 0, 0))
    per_b = lambda shape: pl.BlockSpec((1,) + shape, lambda b, s, pt: (b, 0, 0))
    grid_spec = pltpu.PrefetchScalarGridSpec(
        num_scalar_prefetch=1,
        grid=(bsz, n_pages // pp),
        in_specs=[per_b((rows, aw)), per_b((n_new, aw)), per_b((n_new, aw)),
                  pl.BlockSpec((1, heads, pp * PAGE_SIZE), lambda b, s, pt: (b, 0, s)),
                  per_b((heads, LANES))]
                 + [page(i) for i in range(pp)] * 2,
        out_specs=per_b((n_new, aw)),
        scratch_shapes=[pltpu.VMEM((rows, 1), F32), pltpu.VMEM((rows, 1), F32), pltpu.VMEM((rows, aw), F32)],
    )
    return pl.pallas_call(
        functools.partial(_fox_dec_kernel, pp=pp, n_new=n_new, heads=heads),
        grid_spec=grid_spec,
        out_shape=jax.ShapeDtypeStruct((bsz, n_new, aw), F32),
        compiler_params=_params("parallel", "arbitrary"),
        name="fox_decode_attn",
    )(page_table.reshape(-1), qbd, k_new, v_new, f_past, fn, *([cache_k] * pp), *([cache_v] * pp))


def _fold_own_group(acc, n_groups):
    rows, w = acc.shape
    own = (lax.broadcasted_iota(jnp.int32, (rows, w), 0) % n_groups
           == lax.broadcasted_iota(jnp.int32, (rows, w), 1) // HEAD_DIM)
    a = jnp.where(own, acc, 0.0)
    out = a[:, 0:HEAD_DIM]
    for g in range(1, n_groups):
        out = out + a[:, g * HEAD_DIM:(g + 1) * HEAD_DIM]
    return out


def _nsa_dec_cmp_kernel(pt_ref, qbd_ref, kn_ref, vn_ref, wt_ref, e_ref, *rest, n_pages, n_new, past):
    k_refs, v_refs = rest[:n_pages], rest[n_pages:2 * n_pages]
    oc_ref, sel_ref = rest[2 * n_pages:]
    g = NSA_KV_HEADS
    rows = qbd_ref.shape[1]
    wt = wt_ref[...]
    blk_new = past // NSA_BLOCK

    def summaries(refs, new_ref):
        acc = jnp.zeros((wt.shape[0], LANES), F32)
        for p in range(n_pages):
            x = refs[p][0, 0] * wt
            hi = x.astype(BF16)
            lo = (x - hi.astype(F32)).astype(BF16)
            e_p = e_ref[:, p * PAGE_SIZE:(p + 1) * PAGE_SIZE]
            acc = acc + _dot_nt(hi, e_p) + _dot_nt(lo, e_p)
        new_t = new_ref[0]
        col = new_t[:, 0:1] * wt[:, 0:1]
        for j in range(1, n_new):
            col = col + new_t[:, j:j + 1] * wt[:, j:j + 1]
        lane = lax.broadcasted_iota(jnp.int32, acc.shape, 1)
        return jnp.where(lane == blk_new, col, acc)

    kcs_t = summaries(k_refs, kn_ref)
    vcs_t = summaries(v_refs, vn_ref)
    s = _dot(qbd_ref[0], kcs_t.astype(BF16))
    blk = lax.broadcasted_iota(jnp.int32, (rows, LANES), 1)
    qpos = past + (lax.broadcasted_iota(jnp.int32, (rows, LANES), 0) % (n_new * g)) // g
    pn = _masked_softmax(s, (blk + 1) * NSA_BLOCK - 1 <= qpos)
    oc_ref[0] = _fold_own_group(_dot_nt(pn.astype(BF16), vcs_t.astype(BF16)), g)
    qg = n_new * g
    pn_sum = pn[0:qg]
    for r in range(1, NSA_GROUP):
        pn_sum = pn_sum + pn[r * qg:(r + 1) * qg]
    blk_q = lax.broadcasted_iota(jnp.int32, (qg, LANES), 1)
    qpos_q = past + lax.broadcasted_iota(jnp.int32, (qg, LANES), 0) // g
    score = _block_scores(pn_sum, blk_q, qpos_q)
    sel = _top_n_mask(score, blk_q.astype(F32), NSA_TOP_N)
    sel_ref[0] = _tile_rows(sel, NSA_GROUP).astype(sel_ref.dtype)


def _nsa_dec_attn_kernel(pt_ref, qbd_ref, sel_ref, e_ref, ksn_ref, vsn_ref, kwn_ref, vwn_ref,
                         wk_ref, wv_ref, *rest, n_pages, n_new, past):
    k_refs, v_refs = rest[:n_pages], rest[n_pages:2 * n_pages]
    os_ref, ow_ref = rest[2 * n_pages:]
    g = NSA_KV_HEADS
    qbd = qbd_ref[0]
    rows, w = qbd.shape
    qf = qbd.astype(F32)
    sel = sel_ref[0]
    rowq = (lax.broadcasted_iota(jnp.int32, (rows, 1), 0) % (n_new * g)) // g
    empty = (jnp.full((rows, 1), NEG_BIG, F32), jnp.zeros((rows, 1), F32), jnp.zeros((rows, w), F32))

    state = empty
    for p in range(n_pages):
        s = _dot(qbd, k_refs[p][0, 0].astype(BF16))
        chosen = _dot(sel, e_ref[:, p * PAGE_SIZE:(p + 1) * PAGE_SIZE])
        state = _block_update(state, s + (chosen - 1.0) * (-NEG_BIG), v_refs[p][0, 0].astype(BF16))
    blk_new = past // NSA_BLOCK
    new_sel = sel[:, blk_new:blk_new + 1].astype(F32) > 0.5
    ksn = ksn_ref[0]
    cols = []
    for j in range(n_new):
        sj = jnp.sum(qf * ksn[j:j + 1, :], axis=-1, keepdims=True)
        cols.append(jnp.where((j <= rowq) & new_sel, sj, NEG_BIG))
    m, l, acc = _column_update(state, cols, vsn_ref[0])
    os_ref[0] = _fold_own_group(acc / l, g)

    kwn = kwn_ref[0]
    cols = []
    for j in range(n_new):
        sj = jnp.sum(qf * kwn[j:j + 1, :], axis=-1, keepdims=True)
        cols.append(jnp.where(j <= rowq, sj, NEG_BIG))
    state = _column_update(empty, cols, vwn_ref[0])
    wb = wk_ref.shape[3]
    s = _dot(qbd, wk_ref[0, 0].astype(BF16))
    slot = lax.broadcasted_iota(jnp.int32, (rows, wb), 1)
    dist = wb + rowq - slot
    vis = (dist >= 0) & (dist < NSA_WINDOW) & (past - wb + slot >= 0)
    m, l, acc = _block_update(state, jnp.where(vis, s, NEG_BIG), wv_ref[0, 0].astype(BF16))
    ow_ref[0] = _fold_own_group(acc / l, g)


def nsa_decode(pr, caches, win_k, win_v, layer, page_table, w_cmp, past):
    cmp_k, cmp_v, slc_k, slc_v = caches
    bsz, n_pages = page_table.shape
    kvw = NSA_KV_HEADS * HEAD_DIM
    n_new = pr["kc"].shape[0] // bsz
    heads = pr["qraw"].shape[1] // HEAD_DIM
    rows = n_new * heads
    assert past % NSA_BLOCK == 0 and n_new <= NSA_BLOCK and past // NSA_BLOCK < LANES
    new = lambda name: pr[name].reshape(bsz, n_new, kvw)
    new_t = lambda name: jnp.swapaxes(new(name), 1, 2)
    qbd = lambda name: _block_diag_queries(
        pr[name].reshape(bsz, n_new, heads, HEAD_DIM), NSA_KV_HEADS, "rqg")
    wt = jnp.tile(jnp.repeat(w_cmp, HEAD_DIM, axis=1).T, (1, PAGE_SIZE // NSA_BLOCK))
    page = lambda i: pl.BlockSpec((1, 1, kvw, PAGE_SIZE), lambda b, pt: (layer, pt[b * n_pages + i], 0, 0))
    per_b = lambda shape: pl.BlockSpec((1,) + shape, lambda b, pt: (b, 0, 0))
    const = lambda shape: pl.BlockSpec(shape, lambda b, pt: (0,) * len(shape))
    pt_flat = page_table.reshape(-1)
    sds = jax.ShapeDtypeStruct
    n_tok = n_pages * PAGE_SIZE
    expand = (jnp.arange(LANES)[:, None] == jnp.arange(n_tok)[None, :] // NSA_BLOCK).astype(BF16)

    oc, sel = pl.pallas_call(
        functools.partial(_nsa_dec_cmp_kernel, n_pages=n_pages, n_new=n_new, past=past),
        grid_spec=pltpu.PrefetchScalarGridSpec(
            num_scalar_prefetch=1, grid=(bsz,),
            in_specs=[per_b((rows, kvw)), per_b((kvw, n_new)), per_b((kvw, n_new)), const(wt.shape),
                      const(expand.shape)]
                     + [page(i) for i in range(n_pages)] * 2,
            out_specs=[per_b((rows, HEAD_DIM)), per_b((rows, LANES))]),
        out_shape=[sds((bsz, rows, HEAD_DIM), F32), sds((bsz, rows, LANES), BF16)],
        compiler_params=_params("parallel"),
        name="nsa_decode_cmp_select",
    )(pt_flat, qbd("qraw"), new_t("kc"), new_t("vc"), wt, expand, *([cmp_k] * n_pages), *([cmp_v] * n_pages))

    wb = win_k.shape[3]
    win = pl.BlockSpec((1, 1, kvw, wb), lambda b, pt: (layer, b, 0, 0))
    o_s, o_w = pl.pallas_call(
        functools.partial(_nsa_dec_attn_kernel, n_pages=n_pages, n_new=n_new, past=past),
        grid_spec=pltpu.PrefetchScalarGridSpec(
            num_scalar_prefetch=1, grid=(bsz,),
            in_specs=[per_b((rows, kvw)), per_b((rows, LANES)), const(expand.shape)]
                     + [per_b((n_new, kvw))] * 4 + [win, win]
                     + [page(i) for i in range(n_pages)] * 2,
            out_specs=[per_b((rows, HEAD_DIM))] * 2),
        out_shape=[sds((bsz, rows, HEAD_DIM), F32)] * 2,
        compiler_params=_params("parallel"),
        name="nsa_decode_attn",
    )(pt_flat, qbd("qrot"), sel, expand, new("ks"), new("vs"), new("kw"), new("vw"),
      win_k, win_v, *([slc_k] * n_pages), *([slc_v] * n_pages))

    def to_tokens(o):
        o = o.reshape(bsz, NSA_GROUP, n_new, NSA_KV_HEADS, HEAD_DIM)
        return o.transpose(0, 2, 3, 1, 4).reshape(bsz * n_new, heads * HEAD_DIM)

    return to_tokens(oc), to_tokens(o_s), to_tokens(o_w)


def _rope_tables(pos):
    half = HEAD_DIM // 2
    inv_freq = ROPE_THETA ** (-jnp.arange(half, dtype=F32) / half)
    ang = pos.astype(F32)[:, None] * inv_freq[None, :]
    cos, sin = jnp.cos(ang), jnp.sin(ang)
    reps = LANES // HEAD_DIM
    return (jnp.tile(cos, (1, 2 * reps)), jnp.tile(jnp.concatenate([-sin, sin], axis=1), (1, reps)))


def _expand_gates(gate, heads):
    m = gate.shape[0]
    g = jnp.repeat(gate.reshape(m, 3, heads), HEAD_DIM, axis=-1)
    return g[:, 0], g[:, 1], g[:, 2]


def kernel(x_prompt, x_sample, cache_fox_k, cache_fox_v, cache_fox_logf, cache_nsa_cmp_k, cache_nsa_cmp_v, cache_nsa_slc_k, cache_nsa_slc_v, state_nsa_win_k, state_nsa_win_v, page_table, fox_w_in, fox_b_f, fox_w_o, nsa_w_in, nsa_b_gate, nsa_w_cmp, nsa_w_o, ffn_w_gu, ffn_w_down, ln_mix_g, ln_mix_b, ln_ffn_g, ln_ffn_b):
    batch, seq, d = x_prompt.shape
    dec_b, dec_t, _ = x_sample.shape
    depth = ffn_w_gu.shape[0]
    heads = fox_b_f.shape[1]
    n_pages = page_table.shape[1]
    past = n_pages * PAGE_SIZE
    alpha = (2 * depth) ** 0.25
    n_phys = cache_fox_k.shape[1]
    kvw = NSA_KV_HEADS * HEAD_DIM

    yp = x_prompt.reshape(batch * seq, d)
    ys = x_sample.reshape(dec_b * dec_t, d)
    key_minor = lambda c: jnp.transpose(c, (0, 1, 3, 4, 2)).reshape(c.shape[0], c.shape[1], -1, c.shape[2])
    fox_k_pool, fox_v_pool = key_minor(cache_fox_k), key_minor(cache_fox_v)
    nsa_pools = [key_minor(c) for c in (cache_nsa_cmp_k, cache_nsa_cmp_v, cache_nsa_slc_k, cache_nsa_slc_v)]
    wb = state_nsa_win_k.shape[2]
    win_k, win_v = key_minor(state_nsa_win_k), key_minor(state_nsa_win_v)
    cos_p, sin_p = _rope_tables(jnp.tile(jnp.arange(seq), batch))
    cos_s, sin_s = _rope_tables(jnp.tile(past + jnp.arange(dec_t), dec_b))

    fox_new_p, fox_new_s, nsa_new_p, nsa_new_s = [], [], [], []
    for i in range(depth):
        j = i // 2
        if i % 2 == 0:
            w_in, b_f, w_o = fox_w_in[j], fox_b_f[j], fox_w_o[j]
            q, k, v, kb, vb, lf = fox_project(yp, w_in, b_f)
            lf3 = lf.reshape(batch, seq, heads)
            o = fox_prompt_attention(q, kb, vb, jnp.cumsum(lf3, axis=1), batch)
            fox_new_p.append((k.reshape(batch, seq, heads, HEAD_DIM), v.reshape(batch, seq, heads, HEAD_DIM), lf3))
            yp = wo_post_norm(yp, [o], w_o, ln_mix_g[i], ln_mix_b[i], alpha)

            q, k, v, _, _, lf = fox_project(ys, w_in, b_f)
            lf3 = lf.reshape(dec_b, dec_t, heads)
            lf_past = cache_fox_logf[j][page_table].reshape(dec_b, past, heads).astype(F32)
            f_cum = jnp.swapaxes(jnp.cumsum(jnp.concatenate([lf_past, lf3], axis=1), axis=1), 1, 2)
            o = fox_decode_attention(q.reshape(dec_b, dec_t, -1), k.reshape(dec_b, dec_t, -1),
                                     v.reshape(dec_b, dec_t, -1), f_cum[:, :, :past], f_cum[:, :, past:],
                                     fox_k_pool, fox_v_pool, j, page_table)
            fox_new_s.append((k.reshape(dec_b, dec_t, heads, HEAD_DIM), v.reshape(dec_b, dec_t, heads, HEAD_DIM), lf3))
            ys = wo_post_norm(ys, [o.reshape(dec_b * dec_t, -1)], w_o, ln_mix_g[i], ln_mix_b[i], alpha)
        else:
            w_in, b_g, w_c, w_o = nsa_w_in[j], nsa_b_gate[j], nsa_w_cmp[j], nsa_w_o[j]
            pr = nsa_project(yp, w_in, b_g, w_c, cos_p, sin_p)
            oc, sel = nsa_prompt_compress_select(pr["qraw"], pr["kcsd"], pr["vcsd"], batch)
            o_s = nsa_prompt_selected(pr["qrot"], pr["ksd"], pr["vsd"], sel, batch)
            o_w = nsa_prompt_window(pr["qrot"], pr["kwd"], pr["vwd"], batch)
            shp = (batch, seq, NSA_KV_HEADS, HEAD_DIM)
            keep = min(NSA_WINDOW, seq)
            nsa_new_p.append(tuple(pr[n].reshape(shp) for n in ("kc", "vc", "ks", "vs"))
                             + tuple(pr[n].reshape(shp)[:, seq - keep:] for n in ("kw", "vw")))
            yp = wo_post_norm(yp, [oc, o_s, o_w, *_expand_gates(pr["gate"], heads)], w_o,
                              ln_mix_g[i], ln_mix_b[i], alpha)

            pr = nsa_project(ys, w_in, b_g, w_c, cos_s, sin_s)
            oc, o_s, o_w = nsa_decode(pr, nsa_pools, win_k, win_v, j, page_table, w_c, past)
            shp = (dec_b, dec_t, NSA_KV_HEADS, HEAD_DIM)
            keep = min(NSA_WINDOW, wb + dec_t)
            kw_all = jnp.concatenate([state_nsa_win_k[j], pr["kw"].reshape(shp)], axis=1)
            vw_all = jnp.concatenate([state_nsa_win_v[j], pr["vw"].reshape(shp)], axis=1)
            nsa_new_s.append(tuple(pr[n].reshape(shp) for n in ("kc", "vc", "ks", "vs"))
                             + (kw_all[:, wb + dec_t - keep:], vw_all[:, wb + dec_t - keep:]))
            ys = wo_post_norm(ys, [oc, o_s, o_w, *_expand_gates(pr["gate"], heads)], w_o,
                              ln_mix_g[i], ln_mix_b[i], alpha)
        yp = ffn_post_norm(yp, ffn_w_gu[i], ffn_w_down[i], ln_ffn_g[i], ln_ffn_b[i], alpha)
        ys = ffn_post_norm(ys, ffn_w_gu[i], ffn_w_down[i], ln_ffn_g[i], ln_ffn_b[i], alpha)

    fp = [jnp.stack(a) for a in zip(*fox_new_p)]
    fs = [jnp.stack(a) for a in zip(*fox_new_s)]
    sp = [jnp.stack(a) for a in zip(*nsa_new_p)]
    ss = [jnp.stack(a) for a in zip(*nsa_new_s)]
    return (yp.reshape(batch, seq, d), ys.reshape(dec_b, dec_t, d), fp[0], fs[0], fp[1], fs[1], fp[2], fs[2],
            sp[0], ss[0], sp[1], ss[1], sp[2], ss[2], sp[3], ss[3], sp[4], ss[4], sp[5], ss[5])
```

```python
import functools

import jax
import jax.numpy as jnp
from jax import lax
from jax.experimental import pallas as pl
from jax.experimental.pallas import tpu as pltpu

F32 = jnp.float32
BF16 = jnp.bfloat16

LANES = 128
HEAD_DIM = 64
NSA_KV_HEADS = 4
NSA_GROUP = 4
NSA_BLOCK = 64
NSA_TOP_N = 16
NSA_WINDOW = 512
PAGE_SIZE = 128
ROPE_THETA = 10000.0
LN_EPS = 1e-5
NEG_BIG = -1e30
SEL_FORCE = 1e9
VMEM_LIMIT_BYTES = 56 * 1024 * 1024

TOKEN_TILE = 512
FOX_TILE = 512
NSA_Q_TILE = 256
NSA_K_TILE = 512
NSA_CMP_TILE = 512
DEC_PAGES_PER_STEP = 16


def _dot(a, b):
    return jnp.dot(a, b, preferred_element_type=F32)


def _dot_nt(a, b):
    return lax.dot_general(a, b, (((1,), (1,)), ((), ())), preferred_element_type=F32)


def _params(*sem):
    return pltpu.CompilerParams(dimension_semantics=sem, vmem_limit_bytes=VMEM_LIMIT_BYTES)


def _layer_norm(x, g, b):
    mu = jnp.mean(x, axis=-1, keepdims=True)
    xc = x - mu
    var = jnp.mean(xc * xc, axis=-1, keepdims=True)
    return xc * lax.rsqrt(var + LN_EPS) * g + b


def _full(shape):
    n = len(shape)
    return pl.BlockSpec(shape, lambda *_: (0,) * n)


def _fox_proj_kernel(x_ref, w_ref, wf_ref, bf_ref, q_ref, k_ref, v_ref, kb_ref, vb_ref, lf_ref):
    aw = q_ref.shape[1]
    x = x_ref[...].astype(BF16)
    q_ref[...] = _dot(x, w_ref[:, 0:aw]).astype(BF16)
    k = _dot(x, w_ref[:, aw:2 * aw])
    k_ref[...] = k
    kb_ref[...] = k.astype(BF16)
    v = _dot(x, w_ref[:, 2 * aw:3 * aw])
    v_ref[...] = v
    vb_ref[...] = v.astype(BF16)
    z = _dot(x, wf_ref[...]) + bf_ref[...]
    lf = jnp.minimum(z, 0.0) - jnp.log1p(jnp.exp(-jnp.abs(z)))
    lf_ref[...] = lf[:, :lf_ref.shape[1]]


def fox_project(x, w_in, b_f):
    m, d = x.shape
    nh = b_f.shape[0]
    aw = (w_in.shape[1] - nh) // 3
    scale = HEAD_DIM ** -0.5
    w_qkv = jnp.concatenate([w_in[:, :aw] * scale, w_in[:, aw:3 * aw]], axis=1).astype(BF16)
    w_f = jnp.pad(w_in[:, 3 * aw:], ((0, 0), (0, LANES - nh))).astype(BF16)
    b = jnp.pad(b_f, (0, LANES - nh)).reshape(1, LANES)
    tm = min(TOKEN_TILE, m)
    row = lambda w: pl.BlockSpec((tm, w), lambda i: (i, 0))
    return pl.pallas_call(
        _fox_proj_kernel,
        grid=(m // tm,),
        in_specs=[row(d), _full(w_qkv.shape), _full(w_f.shape), _full(b.shape)],
        out_specs=[row(aw), row(aw), row(aw), row(aw), row(aw), row(nh)],
        out_shape=[jax.ShapeDtypeStruct((m, aw), BF16), jax.ShapeDtypeStruct((m, aw), F32),
                   jax.ShapeDtypeStruct((m, aw), F32), jax.ShapeDtypeStruct((m, aw), BF16),
                   jax.ShapeDtypeStruct((m, aw), BF16), jax.ShapeDtypeStruct((m, nh), F32)],
        compiler_params=_params("parallel"),
        name="fox_proj",
    )(x, w_qkv, w_f, b)


def _rope(x, cos, sin_signed, lo_half):
    outs = []
    for c in range(x.shape[1] // LANES):
        xc = x[:, c * LANES:(c + 1) * LANES]
        swapped = jnp.where(lo_half, pltpu.roll(xc, LANES - HEAD_DIM // 2, 1),
                            pltpu.roll(xc, HEAD_DIM // 2, 1))
        outs.append(xc * cos + swapped * sin_signed)
    return outs


def _nsa_proj_kernel(x_ref, w_ref, bg_ref, wc_ref, cos_ref, sin_ref,
                     qraw_ref, qrot_ref, kc_ref, vc_ref, ks_ref, vs_ref, kw_ref, vw_ref,
                     ksd_ref, vsd_ref, kwd_ref, vwd_ref, kcsd_ref, vcsd_ref, gate_ref):
    aw = qraw_ref.shape[1]
    kvw = kc_ref.shape[1]
    dw = ksd_ref.shape[1]
    tm = x_ref.shape[0]
    x = x_ref[...].astype(BF16)
    cos = cos_ref[...]
    sin = sin_ref[...]
    lane = lax.broadcasted_iota(jnp.int32, (tm, LANES), 1)
    lo_half = (lane % HEAD_DIM) < (HEAD_DIM // 2)

    def rope_store(val, refs):
        for c, piece in enumerate(_rope(val, cos, sin, lo_half)):
            for r in refs:
                r[:, c * LANES:(c + 1) * LANES] = piece.astype(r.dtype)

    q = _dot(x, w_ref[:, 0:aw])
    qraw_ref[...] = q.astype(BF16)
    rope_store(q, [qrot_ref])
    off = aw
    wc = wc_ref[...]
    nblk = tm // NSA_BLOCK

    def summary(val, out_ref):
        s = jnp.sum(val.reshape(nblk, NSA_BLOCK, kvw) * wc[None], axis=1)
        for g in range(NSA_KV_HEADS):
            sg = s[:, g * HEAD_DIM:(g + 1) * HEAD_DIM].astype(out_ref.dtype)
            out_ref[:, g * LANES:g * LANES + HEAD_DIM] = sg
            out_ref[:, g * LANES + HEAD_DIM:(g + 1) * LANES] = sg

    kc = _dot(x, w_ref[:, off:off + kvw])
    kc_ref[...] = kc
    summary(kc, kcsd_ref)
    vc = _dot(x, w_ref[:, off + kvw:off + 2 * kvw])
    vc_ref[...] = vc
    summary(vc, vcsd_ref)
    rope_store(_dot(x, w_ref[:, off + 2 * kvw:off + 3 * kvw]), [ks_ref])
    vs_ref[...] = _dot(x, w_ref[:, off + 3 * kvw:off + 4 * kvw])
    rope_store(_dot(x, w_ref[:, off + 4 * kvw:off + 5 * kvw]), [kw_ref])
    vw_ref[...] = _dot(x, w_ref[:, off + 5 * kvw:off + 6 * kvw])
    off += 6 * kvw
    rope_store(_dot(x, w_ref[:, off:off + dw]), [ksd_ref])
    vsd_ref[...] = _dot(x, w_ref[:, off + dw:off + 2 * dw]).astype(BF16)
    rope_store(_dot(x, w_ref[:, off + 2 * dw:off + 3 * dw]), [kwd_ref])
    vwd_ref[...] = _dot(x, w_ref[:, off + 3 * dw:off + 4 * dw]).astype(BF16)
    off += 4 * dw
    z = _dot(x, w_ref[:, off:off + LANES]) + bg_ref[...]
    gate_ref[...] = 1.0 / (1.0 + jnp.exp(-z))


def _dup_heads(w):
    d, n = w.shape
    g = n // HEAD_DIM
    w = w.reshape(d, g, 1, HEAD_DIM)
    return jnp.broadcast_to(w, (d, g, 2, HEAD_DIM)).reshape(d, g * LANES)


def nsa_project(x, w_in, b_gate, w_cmp, cos_t, sin_t):
    m, d = x.shape
    ng = b_gate.shape[0]
    aw = (ng // 3) * HEAD_DIM
    kvw = NSA_KV_HEADS * HEAD_DIM
    dw = NSA_KV_HEADS * LANES
    scale = HEAD_DIM ** -0.5
    kv = w_in[:, aw:aw + 6 * kvw]
    cols = [w_in[:, :aw] * scale, kv]
    for i in (2, 3, 4, 5):
        cols.append(_dup_heads(kv[:, i * kvw:(i + 1) * kvw]))
    cols.append(jnp.pad(w_in[:, aw + 6 * kvw:], ((0, 0), (0, LANES - ng))))
    w_all = jnp.concatenate(cols, axis=1).astype(BF16)
    bg = jnp.pad(b_gate, (0, LANES - ng)).reshape(1, LANES)
    wc = jnp.repeat(w_cmp, HEAD_DIM, axis=1)
    tm = min(TOKEN_TILE, m)
    row = lambda w: pl.BlockSpec((tm, w), lambda i: (i, 0))
    sds = jax.ShapeDtypeStruct
    nblk = tm // NSA_BLOCK
    outs = pl.pallas_call(
        _nsa_proj_kernel,
        grid=(m // tm,),
        in_specs=[row(d), _full(w_all.shape), _full(bg.shape), _full(wc.shape), row(LANES), row(LANES)],
        out_specs=[row(aw), row(aw)] + [row(kvw)] * 6 + [row(dw)] * 4
                  + [pl.BlockSpec((nblk, dw), lambda i: (i, 0))] * 2 + [row(LANES)],
        out_shape=[sds((m, aw), BF16), sds((m, aw), BF16)] + [sds((m, kvw), F32)] * 6
                  + [sds((m, dw), BF16)] * 4 + [sds((m // NSA_BLOCK, dw), BF16)] * 2 + [sds((m, LANES), F32)],
        compiler_params=_params("parallel"),
        name="nsa_proj",
    )(x, w_all, bg, wc, cos_t, sin_t)
    names = ("qraw", "qrot", "kc", "vc", "ks", "vs", "kw", "vw", "ksd", "vsd", "kwd", "vwd", "kcsd", "vcsd", "gate")
    return dict(zip(names, outs))


def _wo_ln_kernel(x_ref, *rest, gated, alpha):
    if gated:
        oc, os_, ow, gate_ref, e_ref, w_ref, g_ref, b_ref, y_ref = rest
        gate = gate_ref[...]
        hi = gate.astype(BF16)
        lo = (gate - hi.astype(F32)).astype(BF16)
        ge = _dot(hi, e_ref[...]) + _dot(lo, e_ref[...])
        aw = oc.shape[1]
        o = ge[:, 0:aw] * oc[...] + ge[:, aw:2 * aw] * os_[...] + ge[:, 2 * aw:3 * aw] * ow[...]
    else:
        o_ref, w_ref, g_ref, b_ref, y_ref = rest
        o = o_ref[...]
    h = _dot(o.astype(BF16), w_ref[...])
    y_ref[...] = _layer_norm(alpha * x_ref[...] + h, g_ref[...], b_ref[...])


def wo_post_norm(x, branches, w_o, g, b, alpha):
    m, d = x.shape
    aw = w_o.shape[0]
    tm = min(TOKEN_TILE, m)
    row = lambda w: pl.BlockSpec((tm, w), lambda i: (i, 0))
    gated = len(branches) > 1
    extra, extra_specs = [], []
    if gated:
        n_gate = 3 * aw // HEAD_DIM
        expand = (jnp.arange(LANES)[:, None] == jnp.arange(n_gate * HEAD_DIM)[None, :] // HEAD_DIM).astype(BF16)
        extra, extra_specs = [expand], [_full(expand.shape)]
    return pl.pallas_call(
        functools.partial(_wo_ln_kernel, gated=gated, alpha=alpha),
        grid=(m // tm,),
        in_specs=[row(d)] + [row(a.shape[1]) for a in branches] + extra_specs
                 + [_full(w_o.shape), _full((1, d)), _full((1, d))],
        out_specs=row(d),
        out_shape=jax.ShapeDtypeStruct((m, d), F32),
        compiler_params=_params("parallel"),
        name="wo_post_norm",
    )(x, *branches, *extra, w_o.astype(BF16), g.reshape(1, d), b.reshape(1, d))


def _ffn_kernel(x_ref, wg_ref, wu_ref, wd_ref, g_ref, b_ref, y_ref, acc_ref, *, alpha):
    f = pl.program_id(1)
    x = x_ref[...]
    xb = x.astype(BF16)
    gate = _dot(xb, wg_ref[...])
    up = _dot(xb, wu_ref[...])
    h = gate / (1.0 + jnp.exp(-gate)) * up
    part = _dot(h.astype(BF16), wd_ref[...])

    @pl.when(f == 0)
    def _():
        acc_ref[...] = part

    @pl.when(f > 0)
    def _():
        acc_ref[...] += part

    @pl.when(f == pl.num_programs(1) - 1)
    def _():
        y_ref[...] = _layer_norm(alpha * x + acc_ref[...], g_ref[...], b_ref[...])


def ffn_post_norm(x, w_gu, w_down, g, b, alpha):
    m, d = x.shape
    dff = w_down.shape[0]
    nf = 2
    tf = dff // nf
    tm = min(TOKEN_TILE, m)
    w_gu = w_gu.astype(BF16)
    return pl.pallas_call(
        functools.partial(_ffn_kernel, alpha=alpha),
        grid=(m // tm, nf),
        in_specs=[pl.BlockSpec((tm, d), lambda i, f: (i, 0)),
                  pl.BlockSpec((d, tf), lambda i, f: (0, f)),
                  pl.BlockSpec((d, tf), lambda i, f: (0, nf + f)),
                  pl.BlockSpec((tf, d), lambda i, f: (f, 0)),
                  pl.BlockSpec((1, d), lambda i, f: (0, 0)),
                  pl.BlockSpec((1, d), lambda i, f: (0, 0))],
        out_specs=pl.BlockSpec((tm, d), lambda i, f: (i, 0)),
        out_shape=jax.ShapeDtypeStruct((m, d), F32),
        scratch_shapes=[pltpu.VMEM((tm, d), F32)],
        compiler_params=_params("parallel", "arbitrary"),
        name="ffn_post_norm",
    )(x, w_gu, w_gu, w_down.astype(BF16), g.reshape(1, d), b.reshape(1, d))


def _stack_heads(q_ref, qs_ref, tq):
    lane = lax.broadcasted_iota(jnp.int32, (tq, LANES), 1)
    lo = lane < HEAD_DIM
    for c in range(q_ref.shape[1] // LANES):
        qc = q_ref[:, c * LANES:(c + 1) * LANES].astype(F32)
        qs_ref[(2 * c) * tq:(2 * c + 1) * tq, 0:LANES] = jnp.where(lo, qc, 0.0).astype(qs_ref.dtype)
        qs_ref[(2 * c + 1) * tq:(2 * c + 2) * tq, 0:LANES] = jnp.where(lo, 0.0, qc).astype(qs_ref.dtype)


def _unstack_heads(vals, o_ref):
    tq = vals.shape[1]
    lane = lax.broadcasted_iota(jnp.int32, (tq, LANES), 1)
    lo = lane < HEAD_DIM
    for c in range(vals.shape[0] // 2):
        o_ref[:, c * LANES:(c + 1) * LANES] = jnp.where(lo, vals[2 * c], vals[2 * c + 1])


def _softmax_init(m_ref, l_ref, acc_ref):
    m_ref[...] = jnp.full(m_ref.shape, NEG_BIG, F32)
    l_ref[...] = jnp.zeros(l_ref.shape, F32)
    acc_ref[...] = jnp.zeros(acc_ref.shape, F32)


def _softmax_update(s, v_tile, m_ref, l_ref, acc_ref):
    tk = s.shape[1]
    m_prev = m_ref[...]
    m_new = jnp.maximum(m_prev, jnp.max(s, axis=-1, keepdims=True))
    alpha = jnp.exp(m_prev - m_new)
    p = jnp.exp((s - jnp.tile(m_new, (1, tk // LANES))).astype(BF16))
    p_sum = p[:, 0:LANES]
    for c in range(1, tk // LANES):
        p_sum = p_sum + p[:, c * LANES:(c + 1) * LANES]
    l_ref[...] = alpha * l_ref[...] + jnp.sum(p_sum.astype(F32), axis=-1, keepdims=True)
    acc_ref[...] = alpha * acc_ref[...] + _dot(p, v_tile)
    m_ref[...] = m_new


def _attn_scratch(rows):
    return [pltpu.VMEM((rows, LANES), BF16), pltpu.VMEM((rows, LANES), F32),
            pltpu.VMEM((rows, LANES), F32), pltpu.VMEM((rows, LANES), F32)]


def _fox_attn_kernel(q_ref, k_ref, v_ref, f_ref, o_ref, qs_ref, m_ref, l_ref, acc_ref, *, t):
    qi = pl.program_id(2)
    _stack_heads(q_ref, qs_ref, t)
    _softmax_init(m_ref, l_ref, acc_ref)
    q0 = pl.multiple_of(qi * t, t)
    f_q0 = f_ref[0, 0, :, pl.ds(q0, LANES)][:, 0:1]

    def tile(kt, causal):
        k0 = pl.multiple_of(kt * t, t)
        s = _dot_nt(qs_ref[...], k_ref[pl.ds(k0, t), :])
        bias = f_q0 - f_ref[0, 0, :, pl.ds(k0, t)]
        s3 = s.reshape(2, t, t) + bias[:, None, :]
        if causal:
            row = lax.broadcasted_iota(jnp.int32, (t, t), 0)
            col = lax.broadcasted_iota(jnp.int32, (t, t), 1)
            s3 = jnp.where((col <= row)[None], s3, NEG_BIG)
        _softmax_update(s3.reshape(2 * t, t), v_ref[pl.ds(k0, t), :], m_ref, l_ref, acc_ref)

    def body(kt, carry):
        tile(kt, False)
        return carry

    lax.fori_loop(0, qi, body, 0)
    tile(qi, True)
    _unstack_heads((acc_ref[...] / l_ref[...]).reshape(2, t, LANES), o_ref)


def fox_prompt_attention(q, k, v, f_cum, batch):
    m, aw = q.shape
    seq = m // batch
    t = FOX_TILE
    nq = seq // t
    npair = aw // LANES
    f = jnp.swapaxes(f_cum, 1, 2).reshape(batch, npair, 2, seq)
    return pl.pallas_call(
        functools.partial(_fox_attn_kernel, t=t),
        grid=(batch, npair, nq),
        in_specs=[pl.BlockSpec((t, LANES), lambda b, c, i: (b * nq + i, c)),
                  pl.BlockSpec((seq, LANES), lambda b, c, i: (b, c)),
                  pl.BlockSpec((seq, LANES), lambda b, c, i: (b, c)),
                  pl.BlockSpec((1, 1, 2, seq), lambda b, c, i: (b, c, 0, 0))],
        out_specs=pl.BlockSpec((t, LANES), lambda b, c, i: (b * nq + i, c)),
        out_shape=jax.ShapeDtypeStruct((m, aw), F32),
        scratch_shapes=_attn_scratch(2 * t),
        compiler_params=_params("parallel", "parallel", "arbitrary"),
        name="fox_prompt_attn",
    )(q, k, v, f)


def _top_n_mask(score, blk_f, n_sel):
    sel = jnp.zeros(score.shape, F32)
    for _ in range(n_sel):
        mx = jnp.max(score, axis=-1, keepdims=True)
        idx = jnp.min(jnp.where(score == mx, blk_f, SEL_FORCE), axis=-1, keepdims=True)
        hit = blk_f == idx
        sel = jnp.where(hit, 1.0, sel)
        score = jnp.where(hit, -jnp.inf, score)
    return sel


def _block_scores(pn_sum, blk, qpos):
    cur = qpos // NSA_BLOCK
    forced = (blk == 0) | (blk == cur) | (blk == cur - 1)
    return jnp.where(blk <= cur, jnp.where(forced, SEL_FORCE, pn_sum), -SEL_FORCE)


def _masked_softmax(s, vis):
    s = jnp.where(vis, s, NEG_BIG)
    p = jnp.where(vis, jnp.exp(s - jnp.max(s, axis=-1, keepdims=True)), 0.0)
    return p / jnp.maximum(jnp.sum(p, axis=-1, keepdims=True), 1e-30)


def _nsa_cmp_kernel(q_ref, kk_ref, vv_ref, oc_ref, sel_ref, qs_ref, *, tq):
    qi = pl.program_id(2)
    nb = kk_ref.shape[0]
    _stack_heads(q_ref, qs_ref, tq)
    s3 = _dot_nt(qs_ref[...], kk_ref[...]).reshape(NSA_GROUP, tq, nb)
    blk = lax.broadcasted_iota(jnp.int32, (tq, nb), 1)
    qpos = qi * tq + lax.broadcasted_iota(jnp.int32, (tq, nb), 0)
    vis = (blk + 1) * NSA_BLOCK - 1 <= qpos
    pn = _masked_softmax(s3, vis[None])
    oc = _dot(pn.reshape(NSA_GROUP * tq, nb).astype(BF16), vv_ref[...])
    _unstack_heads(oc.reshape(NSA_GROUP, tq, LANES), oc_ref)
    score = _block_scores(jnp.sum(pn, axis=0), blk, qpos)
    sel_ref[0, 0] = _top_n_mask(score, blk.astype(F32), min(NSA_TOP_N, nb)).astype(sel_ref.dtype)


def nsa_prompt_compress_select(qraw, kcsd, vcsd, batch):
    m, aw = qraw.shape
    seq = m // batch
    nb = seq // NSA_BLOCK
    assert nb <= LANES, "block axis is mapped onto one vreg lane row"
    tq = NSA_CMP_TILE
    nq = seq // tq
    gw = NSA_GROUP * HEAD_DIM
    ng = aw // gw
    return pl.pallas_call(
        functools.partial(_nsa_cmp_kernel, tq=tq),
        grid=(batch, ng, nq),
        in_specs=[pl.BlockSpec((tq, gw), lambda b, g, i: (b * nq + i, g)),
                  pl.BlockSpec((nb, LANES), lambda b, g, i: (b, g)),
                  pl.BlockSpec((nb, LANES), lambda b, g, i: (b, g))],
        out_specs=[pl.BlockSpec((tq, gw), lambda b, g, i: (b * nq + i, g)),
                   pl.BlockSpec((1, 1, tq, nb), lambda b, g, i: (b, g, i, 0))],
        out_shape=[jax.ShapeDtypeStruct((m, aw), F32), jax.ShapeDtypeStruct((batch, ng, seq, nb), BF16)],
        scratch_shapes=[pltpu.VMEM((NSA_GROUP * tq, LANES), BF16)],
        compiler_params=_params("parallel", "parallel", "parallel"),
        name="nsa_prompt_cmp_select",
    )(qraw, kcsd, vcsd)


def _nsa_sel_kernel(q_ref, k_ref, v_ref, sel_ref, o_ref, qs_ref, m_ref, l_ref, acc_ref, *, tq, tk):
    qi = pl.program_id(2)
    _stack_heads(q_ref, qs_ref, tq)
    _softmax_init(m_ref, l_ref, acc_ref)
    nb = sel_ref.shape[3]
    sel_bias = ((sel_ref[0, 0].astype(F32) - 1.0) * (-NEG_BIG)).astype(qs_ref.dtype)
    for i in range(NSA_GROUP):
        qs_ref[i * tq:(i + 1) * tq, LANES:LANES + nb] = sel_bias
    q0 = qi * tq
    kt_diag = q0 // tk

    def tile(kt, causal):
        k0 = pl.multiple_of(kt * tk, tk)
        s = _dot_nt(qs_ref[...], k_ref[pl.ds(k0, tk), :])
        if causal:
            qpos = q0 + lax.broadcasted_iota(jnp.int32, (tq, tk), 0)
            kpos = k0 + lax.broadcasted_iota(jnp.int32, (tq, tk), 1)
            s = jnp.where((kpos <= qpos)[None], s.reshape(NSA_GROUP, tq, tk), NEG_BIG).reshape(NSA_GROUP * tq, tk)
        _softmax_update(s, v_ref[pl.ds(k0, tk), :], m_ref, l_ref, acc_ref)

    def body(kt, carry):
        tile(kt, False)
        return carry

    lax.fori_loop(0, kt_diag, body, 0)
    tile(kt_diag, True)
    _unstack_heads((acc_ref[...] / l_ref[...]).reshape(NSA_GROUP, tq, LANES), o_ref)


def nsa_prompt_selected(qrot, ksd, vsd, sel, batch):
    m, aw = qrot.shape
    seq = m // batch
    nb = seq // NSA_BLOCK
    tq, tk = NSA_Q_TILE, NSA_K_TILE
    nq = seq // tq
    gw = NSA_GROUP * HEAD_DIM
    ng = aw // gw
    onehot = (jnp.arange(seq)[:, None] // NSA_BLOCK == jnp.arange(nb)[None, :]).astype(BF16)
    k_ext = jnp.concatenate([ksd.reshape(batch, seq, ng, LANES),
                             jnp.broadcast_to(onehot[None, :, None, :], (batch, seq, ng, nb))], axis=-1)
    kw = LANES + nb
    scratch = _attn_scratch(NSA_GROUP * tq)
    scratch[0] = pltpu.VMEM((NSA_GROUP * tq, kw), BF16)
    return pl.pallas_call(
        functools.partial(_nsa_sel_kernel, tq=tq, tk=tk),
        grid=(batch, ng, nq),
        in_specs=[pl.BlockSpec((tq, gw), lambda b, g, i: (b * nq + i, g)),
                  pl.BlockSpec((seq, kw), lambda b, g, i: (b, g)),
                  pl.BlockSpec((seq, LANES), lambda b, g, i: (b, g)),
                  pl.BlockSpec((1, 1, tq, nb), lambda b, g, i: (b, g, i, 0))],
        out_specs=pl.BlockSpec((tq, gw), lambda b, g, i: (b * nq + i, g)),
        out_shape=jax.ShapeDtypeStruct((m, aw), F32),
        scratch_shapes=scratch,
        compiler_params=_params("parallel", "parallel", "arbitrary"),
        name="nsa_prompt_selected",
    )(qrot, k_ext.reshape(m, ng * kw), vsd, sel)


def _nsa_win_kernel(q_ref, k_ref, v_ref, o_ref, qs_ref, *, tq):
    qi = pl.program_id(2)
    span = NSA_WINDOW + tq
    _stack_heads(q_ref, qs_ref, tq)
    k0 = pl.multiple_of(jnp.maximum(qi * tq - NSA_WINDOW, 0), tq)
    s = _dot_nt(qs_ref[...], k_ref[pl.ds(k0, span), :])
    dist = (qi * tq - k0 + lax.broadcasted_iota(jnp.int32, (tq, span), 0)
            - lax.broadcasted_iota(jnp.int32, (tq, span), 1))
    vis = (dist >= 0) & (dist < NSA_WINDOW)
    s = jnp.where(vis[None], s.reshape(NSA_GROUP, tq, span), NEG_BIG).reshape(NSA_GROUP * tq, span)
    p = jnp.exp(s - jnp.max(s, axis=-1, keepdims=True))
    o = _dot(p.astype(BF16), v_ref[pl.ds(k0, span), :]) / jnp.sum(p, axis=-1, keepdims=True)
    _unstack_heads(o.reshape(NSA_GROUP, tq, LANES), o_ref)


def nsa_prompt_window(qrot, kwd, vwd, batch):
    m, aw = qrot.shape
    seq = m // batch
    tq = NSA_Q_TILE
    assert NSA_WINDOW % tq == 0 and seq >= NSA_WINDOW + tq
    nq = seq // tq
    gw = NSA_GROUP * HEAD_DIM
    ng = aw // gw
    return pl.pallas_call(
        functools.partial(_nsa_win_kernel, tq=tq),
        grid=(batch, ng, nq),
        in_specs=[pl.BlockSpec((tq, gw), lambda b, g, i: (b * nq + i, g)),
                  pl.BlockSpec((seq, LANES), lambda b, g, i: (b, g)),
                  pl.BlockSpec((seq, LANES), lambda b, g, i: (b, g))],
        out_specs=pl.BlockSpec((tq, gw), lambda b, g, i: (b * nq + i, g)),
        out_shape=jax.ShapeDtypeStruct((m, aw), F32),
        scratch_shapes=[pltpu.VMEM((NSA_GROUP * tq, LANES), BF16)],
        compiler_params=_params("parallel", "parallel", "parallel"),
        name="nsa_prompt_window",
    )(qrot, kwd, vwd)


def _tile_rows(x, n):
    return jnp.concatenate([x] * n, axis=0)


def _column_update(state, cols, v_rows):
    m_prev, l_prev, acc = state
    m_new = m_prev
    for c in cols:
        m_new = jnp.maximum(m_new, c)
    alpha = jnp.exp(m_prev - m_new)
    l_new = alpha * l_prev
    acc = alpha * acc
    for j, c in enumerate(cols):
        p = jnp.exp(c - m_new)
        l_new = l_new + p
        acc = acc + p * v_rows[j:j + 1, :]
    return m_new, l_new, acc


def _block_update(state, s, v_t):
    m_prev, l_prev, acc = state
    m_new = jnp.maximum(m_prev, jnp.max(s, axis=-1, keepdims=True))
    alpha = jnp.exp(m_prev - m_new)
    p = jnp.exp(s - m_new)
    l_new = alpha * l_prev + jnp.sum(p, axis=-1, keepdims=True)
    acc = alpha * acc + _dot_nt(p.astype(BF16), v_t)
    return m_new, l_new, acc


def _fox_dec_kernel(pt_ref, qbd_ref, kn_ref, vn_ref, fp_ref, fn_ref, *rest, pp, n_new, heads):
    k_refs, v_refs = rest[:pp], rest[pp:2 * pp]
    o_ref, m_ref, l_ref, acc_ref = rest[2 * pp:]
    step = pl.program_id(1)
    rows = n_new * heads
    qbd = qbd_ref[0]
    fn = fn_ref[0]
    f_ref0 = fn[:, 0:1]

    @pl.when(step == 0)
    def _():
        qf = qbd.astype(F32)
        kn = kn_ref[0]
        rowq = lax.broadcasted_iota(jnp.int32, (rows, 1), 0) // heads
        cols = []
        for j in range(n_new):
            sj = jnp.sum(qf * kn[j:j + 1, :], axis=-1, keepdims=True)
            sj = sj + _tile_rows(f_ref0 - fn[:, j:j + 1], n_new)
            cols.append(jnp.where(j <= rowq, sj, NEG_BIG))
        init = (jnp.full((rows, 1), NEG_BIG, F32), jnp.zeros((rows, 1), F32),
                jnp.zeros(acc_ref.shape, F32))
        m, l, acc = _column_update(init, cols, vn_ref[0])
        m_ref[...] = m
        l_ref[...] = l
        acc_ref[...] = acc

    state = (m_ref[...], l_ref[...], acc_ref[...])
    for i in range(pp):
        s = _dot(qbd, k_refs[i][0, 0].astype(BF16))
        bias = f_ref0 - fp_ref[0][:, i * PAGE_SIZE:(i + 1) * PAGE_SIZE]
        state = _block_update(state, s + _tile_rows(bias, n_new), v_refs[i][0, 0].astype(BF16))
    m_ref[...], l_ref[...], acc_ref[...] = state

    @pl.when(step == pl.num_programs(1) - 1)
    def _():
        w = acc_ref.shape[1]
        o = acc_ref[...] / l_ref[...]
        own = (lax.broadcasted_iota(jnp.int32, (rows, w), 0) % heads
               == lax.broadcasted_iota(jnp.int32, (rows, w), 1) // HEAD_DIM)
        o_ref[0] = jnp.sum(jnp.where(own, o, 0.0).reshape(n_new, heads, w), axis=1)


def _block_diag_queries(q, n_lane_groups, order):
    bsz, tq, h, dh = q.shape
    r = h // n_lane_groups
    onehot = (jnp.arange(h)[:, None] // r == jnp.arange(n_lane_groups)[None, :]).astype(q.dtype)
    x = q[:, :, :, None, :] * onehot[None, None, :, :, None]
    if order == "rqg":
        x = x.reshape(bsz, tq, n_lane_groups, r, n_lane_groups, dh).transpose(0, 3, 1, 2, 4, 5)
    return x.reshape(bsz, tq * h, n_lane_groups * dh)


def fox_decode_attention(q, k_new, v_new, f_past, f_new, cache_k, cache_v, layer, page_table):
    bsz, n_new, aw = q.shape
    heads = aw // HEAD_DIM
    n_pages = page_table.shape[1]
    pp = min(DEC_PAGES_PER_STEP, n_pages)
    rows = n_new * heads
    qbd = _block_diag_queries(q.reshape(bsz, n_new, heads, HEAD_DIM), heads, "qh")
    fn = jnp.pad(f_new, ((0, 0), (0, 0), (0, LANES - n_new)))
    page = lambda i: pl.BlockSpec(
        (1, 1, aw, PAGE_SIZE), lambda b, s, pt: (layer, pt[b * n_pages + s * pp + i], 0, 0))
    per_b = lambda shape: pl.BlockSpec((1,) + shape, lambda b, s, pt: (b, 0, 0))
    grid_spec = pltpu.PrefetchScalarGridSpec(
        num_scalar_prefetch=1,
        grid=(bsz, n_pages // pp),
        in_specs=[per_b((rows, aw)), per_b((n_new, aw)), per_b((n_new, aw)),
                  pl.BlockSpec((1, heads, pp * PAGE_SIZE), lambda b, s, pt: (b, 0, s)),
                  per_b((heads, LANES))]
                 + [page(i) for i in range(pp)] * 2,
        out_specs=per_b((n_new, aw)),
        scratch_shapes=[pltpu.VMEM((rows, 1), F32), pltpu.VMEM((rows, 1), F32), pltpu.VMEM((rows, aw), F32)],
    )
    return pl.pallas_call(
        functools.partial(_fox_dec_kernel, pp=pp, n_new=n_new, heads=heads),
        grid_spec=grid_spec,
        out_shape=jax.ShapeDtypeStruct((bsz, n_new, aw), F32),
        compiler_params=_params("parallel", "arbitrary"),
        name="fox_decode_attn",
    )(page_table.reshape(-1), qbd, k_new, v_new, f_past, fn, *([cache_k] * pp), *([cache_v] * pp))


def _fold_own_group(acc, n_groups):
    rows, w = acc.shape
    own = (lax.broadcasted_iota(jnp.int32, (rows, w), 0) % n_groups
           == lax.broadcasted_iota(jnp.int32, (rows, w), 1) // HEAD_DIM)
    a = jnp.where(own, acc, 0.0)
    out = a[:, 0:HEAD_DIM]
    for g in range(1, n_groups):
        out = out + a[:, g * HEAD_DIM:(g + 1) * HEAD_DIM]
    return out


def _nsa_dec_cmp_kernel(pt_ref, qbd_ref, kn_ref, vn_ref, wt_ref, e_ref, *rest, n_pages, n_new, past):
    k_refs, v_refs = rest[:n_pages], rest[n_pages:2 * n_pages]
    oc_ref, sel_ref = rest[2 * n_pages:]
    g = NSA_KV_HEADS
    rows = qbd_ref.shape[1]
    wt = wt_ref[...]
    blk_new = past // NSA_BLOCK

    def summaries(refs, new_ref):
        acc = jnp.zeros((wt.shape[0], LANES), F32)
        for p in range(n_pages):
            x = refs[p][0, 0] * wt
            hi = x.astype(BF16)
            lo = (x - hi.astype(F32)).astype(BF16)
            e_p = e_ref[:, p * PAGE_SIZE:(p + 1) * PAGE_SIZE]
            acc = acc + _dot_nt(hi, e_p) + _dot_nt(lo, e_p)
        new_t = new_ref[0]
        col = new_t[:, 0:1] * wt[:, 0:1]
        for j in range(1, n_new):
            col = col + new_t[:, j:j + 1] * wt[:, j:j + 1]
        lane = lax.broadcasted_iota(jnp.int32, acc.shape, 1)
        return jnp.where(lane == blk_new, col, acc)

    kcs_t = summaries(k_refs, kn_ref)
    vcs_t = summaries(v_refs, vn_ref)
    s = _dot(qbd_ref[0], kcs_t.astype(BF16))
    blk = lax.broadcasted_iota(jnp.int32, (rows, LANES), 1)
    qpos = past + (lax.broadcasted_iota(jnp.int32, (rows, LANES), 0) % (n_new * g)) // g
    pn = _masked_softmax(s, (blk + 1) * NSA_BLOCK - 1 <= qpos)
    oc_ref[0] = _fold_own_group(_dot_nt(pn.astype(BF16), vcs_t.astype(BF16)), g)
    qg = n_new * g
    pn_sum = pn[0:qg]
    for r in range(1, NSA_GROUP):
        pn_sum = pn_sum + pn[r * qg:(r + 1) * qg]
    blk_q = lax.broadcasted_iota(jnp.int32, (qg, LANES), 1)
    qpos_q = past + lax.broadcasted_iota(jnp.int32, (qg, LANES), 0) // g
    score = _block_scores(pn_sum, blk_q, qpos_q)
    sel = _top_n_mask(score, blk_q.astype(F32), NSA_TOP_N)
    sel_ref[0] = _tile_rows(sel, NSA_GROUP).astype(sel_ref.dtype)


def _nsa_dec_attn_kernel(pt_ref, qbd_ref, sel_ref, e_ref, ksn_ref, vsn_ref, kwn_ref, vwn_ref,
                         wk_ref, wv_ref, *rest, n_pages, n_new, past):
    k_refs, v_refs = rest[:n_pages], rest[n_pages:2 * n_pages]
    os_ref, ow_ref = rest[2 * n_pages:]
    g = NSA_KV_HEADS
    qbd = qbd_ref[0]
    rows, w = qbd.shape
    qf = qbd.astype(F32)
    sel = sel_ref[0]
    rowq = (lax.broadcasted_iota(jnp.int32, (rows, 1), 0) % (n_new * g)) // g
    empty = (jnp.full((rows, 1), NEG_BIG, F32), jnp.zeros((rows, 1), F32), jnp.zeros((rows, w), F32))

    state = empty
    for p in range(n_pages):
        s = _dot(qbd, k_refs[p][0, 0].astype(BF16))
        chosen = _dot(sel, e_ref[:, p * PAGE_SIZE:(p + 1) * PAGE_SIZE])
        state = _block_update(state, s + (chosen - 1.0) * (-NEG_BIG), v_refs[p][0, 0].astype(BF16))
    blk_new = past // NSA_BLOCK
    new_sel = sel[:, blk_new:blk_new + 1].astype(F32) > 0.5
    ksn = ksn_ref[0]
    cols = []
    for j in range(n_new):
        sj = jnp.sum(qf * ksn[j:j + 1, :], axis=-1, keepdims=True)
        cols.append(jnp.where((j <= rowq) & new_sel, sj, NEG_BIG))
    m, l, acc = _column_update(state, cols, vsn_ref[0])
    os_ref[0] = _fold_own_group(acc / l, g)

    kwn = kwn_ref[0]
    cols = []
    for j in range(n_new):
        sj = jnp.sum(qf * kwn[j:j + 1, :], axis=-1, keepdims=True)
        cols.append(jnp.where(j <= rowq, sj, NEG_BIG))
    state = _column_update(empty, cols, vwn_ref[0])
    wb = wk_ref.shape[3]
    s = _dot(qbd, wk_ref[0, 0].astype(BF16))
    slot = lax.broadcasted_iota(jnp.int32, (rows, wb), 1)
    dist = wb + rowq - slot
    vis = (dist >= 0) & (dist < NSA_WINDOW) & (past - wb + slot >= 0)
    m, l, acc = _block_update(state, jnp.where(vis, s, NEG_BIG), wv_ref[0, 0].astype(BF16))
    ow_ref[0] = _fold_own_group(acc / l, g)


def nsa_decode(pr, caches, win_k, win_v, layer, page_table, w_cmp, past):
    cmp_k, cmp_v, slc_k, slc_v = caches
    bsz, n_pages = page_table.shape
    kvw = NSA_KV_HEADS * HEAD_DIM
    n_new = pr["kc"].shape[0] // bsz
    heads = pr["qraw"].shape[1] // HEAD_DIM
    rows = n_new * heads
    assert past % NSA_BLOCK == 0 and n_new <= NSA_BLOCK and past // NSA_BLOCK < LANES
    new = lambda name: pr[name].reshape(bsz, n_new, kvw)
    new_t = lambda name: jnp.swapaxes(new(name), 1, 2)
    qbd = lambda name: _block_diag_queries(
        pr[name].reshape(bsz, n_new, heads, HEAD_DIM), NSA_KV_HEADS, "rqg")
    wt = jnp.tile(jnp.repeat(w_cmp, HEAD_DIM, axis=1).T, (1, PAGE_SIZE // NSA_BLOCK))
    page = lambda i: pl.BlockSpec((1, 1, kvw, PAGE_SIZE), lambda b, pt: (layer, pt[b * n_pages + i], 0, 0))
    per_b = lambda shape: pl.BlockSpec((1,) + shape, lambda b, pt: (b, 0, 0))
    const = lambda shape: pl.BlockSpec(shape, lambda b, pt: (0,) * len(shape))
    pt_flat = page_table.reshape(-1)
    sds = jax.ShapeDtypeStruct
    n_tok = n_pages * PAGE_SIZE
    expand = (jnp.arange(LANES)[:, None] == jnp.arange(n_tok)[None, :] // NSA_BLOCK).astype(BF16)

    oc, sel = pl.pallas_call(
        functools.partial(_nsa_dec_cmp_kernel, n_pages=n_pages, n_new=n_new, past=past),
        grid_spec=pltpu.PrefetchScalarGridSpec(
            num_scalar_prefetch=1, grid=(bsz,),
            in_specs=[per_b((rows, kvw)), per_b((kvw, n_new)), per_b((kvw, n_new)), const(wt.shape),
                      const(expand.shape)]
                     + [page(i) for i in range(n_pages)] * 2,
            out_specs=[per_b((rows, HEAD_DIM)), per_b((rows, LANES))]),
        out_shape=[sds((bsz, rows, HEAD_DIM), F32), sds((bsz, rows, LANES), BF16)],
        compiler_params=_params("parallel"),
        name="nsa_decode_cmp_select",
    )(pt_flat, qbd("qraw"), new_t("kc"), new_t("vc"), wt, expand, *([cmp_k] * n_pages), *([cmp_v] * n_pages))

    wb = win_k.shape[3]
    win = pl.BlockSpec((1, 1, kvw, wb), lambda b, pt: (layer, b, 0, 0))
    o_s, o_w = pl.pallas_call(
        functools.partial(_nsa_dec_attn_kernel, n_pages=n_pages, n_new=n_new, past=past),
        grid_spec=pltpu.PrefetchScalarGridSpec(
            num_scalar_prefetch=1, grid=(bsz,),
            in_specs=[per_b((rows, kvw)), per_b((rows, LANES)), const(expand.shape)]
                     + [per_b((n_new, kvw))] * 4 + [win, win]
                     + [page(i) for i in range(n_pages)] * 2,
            out_specs=[per_b((rows, HEAD_DIM))] * 2),
        out_shape=[sds((bsz, rows, HEAD_DIM), F32)] * 2,
        compiler_params=_params("parallel"),
        name="nsa_decode_attn",
    )(pt_flat, qbd("qrot"), sel, expand, new("ks"), new("vs"), new("kw"), new("vw"),
      win_k, win_v, *([slc_k] * n_pages), *([slc_v] * n_pages))

    def to_tokens(o):
        o = o.reshape(bsz, NSA_GROUP, n_new, NSA_KV_HEADS, HEAD_DIM)
        return o.transpose(0, 2, 3, 1, 4).reshape(bsz * n_new, heads * HEAD_DIM)

    return to_tokens(oc), to_tokens(o_s), to_tokens(o_w)


def _rope_tables(pos):
    half = HEAD_DIM // 2
    inv_freq = ROPE_THETA ** (-jnp.arange(half, dtype=F32) / half)
    ang = pos.astype(F32)[:, None] * inv_freq[None, :]
    cos, sin = jnp.cos(ang), jnp.sin(ang)
    reps = LANES // HEAD_DIM
    return (jnp.tile(cos, (1, 2 * reps)), jnp.tile(jnp.concatenate([-sin, sin], axis=1), (1, reps)))


def kernel(x_prompt, x_sample, cache_fox_k, cache_fox_v, cache_fox_logf, cache_nsa_cmp_k, cache_nsa_cmp_v, cache_nsa_slc_k, cache_nsa_slc_v, state_nsa_win_k, state_nsa_win_v, page_table, fox_w_in, fox_b_f, fox_w_o, nsa_w_in, nsa_b_gate, nsa_w_cmp, nsa_w_o, ffn_w_gu, ffn_w_down, ln_mix_g, ln_mix_b, ln_ffn_g, ln_ffn_b):
    batch, seq, d = x_prompt.shape
    dec_b, dec_t, _ = x_sample.shape
    depth = ffn_w_gu.shape[0]
    heads = fox_b_f.shape[1]
    n_pages = page_table.shape[1]
    past = n_pages * PAGE_SIZE
    alpha = (2 * depth) ** 0.25
    n_phys = cache_fox_k.shape[1]
    kvw = NSA_KV_HEADS * HEAD_DIM

    yp = x_prompt.reshape(batch * seq, d)
    ys = x_sample.reshape(dec_b * dec_t, d)
    key_minor = lambda c: jnp.transpose(c, (0, 1, 3, 4, 2)).reshape(c.shape[0], c.shape[1], -1, c.shape[2])
    fox_k_pool, fox_v_pool = key_minor(cache_fox_k), key_minor(cache_fox_v)
    nsa_pools = [key_minor(c) for c in (cache_nsa_cmp_k, cache_nsa_cmp_v, cache_nsa_slc_k, cache_nsa_slc_v)]
    wb = state_nsa_win_k.shape[2]
    win_k, win_v = key_minor(state_nsa_win_k), key_minor(state_nsa_win_v)
    cos_p, sin_p = _rope_tables(jnp.tile(jnp.arange(seq), batch))
    cos_s, sin_s = _rope_tables(jnp.tile(past + jnp.arange(dec_t), dec_b))

    fox_new_p, fox_new_s, nsa_new_p, nsa_new_s = [], [], [], []
    for i in range(depth):
        j = i // 2
        if i % 2 == 0:
            w_in, b_f, w_o = fox_w_in[j], fox_b_f[j], fox_w_o[j]
            q, k, v, kb, vb, lf = fox_project(yp, w_in, b_f)
            lf3 = lf.reshape(batch, seq, heads)
            o = fox_prompt_attention(q, kb, vb, jnp.cumsum(lf3, axis=1), batch)
            fox_new_p.append((k.reshape(batch, seq, heads, HEAD_DIM), v.reshape(batch, seq, heads, HEAD_DIM), lf3))
            yp = wo_post_norm(yp, [o], w_o, ln_mix_g[i], ln_mix_b[i], alpha)

            q, k, v, _, _, lf = fox_project(ys, w_in, b_f)
            lf3 = lf.reshape(dec_b, dec_t, heads)
            lf_past = cache_fox_logf[j][page_table].reshape(dec_b, past, heads).astype(F32)
            f_cum = jnp.swapaxes(jnp.cumsum(jnp.concatenate([lf_past, lf3], axis=1), axis=1), 1, 2)
            o = fox_decode_attention(q.reshape(dec_b, dec_t, -1), k.reshape(dec_b, dec_t, -1),
                                     v.reshape(dec_b, dec_t, -1), f_cum[:, :, :past], f_cum[:, :, past:],
                                     fox_k_pool, fox_v_pool, j, page_table)
            fox_new_s.append((k.reshape(dec_b, dec_t, heads, HEAD_DIM), v.reshape(dec_b, dec_t, heads, HEAD_DIM), lf3))
            ys = wo_post_norm(ys, [o.reshape(dec_b * dec_t, -1)], w_o, ln_mix_g[i], ln_mix_b[i], alpha)
        else:
            w_in, b_g, w_c, w_o = nsa_w_in[j], nsa_b_gate[j], nsa_w_cmp[j], nsa_w_o[j]
            pr = nsa_project(yp, w_in, b_g, w_c, cos_p, sin_p)
            oc, sel = nsa_prompt_compress_select(pr["qraw"], pr["kcsd"], pr["vcsd"], batch)
            o_s = nsa_prompt_selected(pr["qrot"], pr["ksd"], pr["vsd"], sel, batch)
            o_w = nsa_prompt_window(pr["qrot"], pr["kwd"], pr["vwd"], batch)
            shp = (batch, seq, NSA_KV_HEADS, HEAD_DIM)
            keep = min(NSA_WINDOW, seq)
            nsa_new_p.append(tuple(pr[n].reshape(shp) for n in ("kc", "vc", "ks", "vs"))
                             + tuple(pr[n].reshape(shp)[:, seq - keep:] for n in ("kw", "vw")))
            yp = wo_post_norm(yp, [oc, o_s, o_w, pr["gate"]], w_o, ln_mix_g[i], ln_mix_b[i], alpha)

            pr = nsa_project(ys, w_in, b_g, w_c, cos_s, sin_s)
            oc, o_s, o_w = nsa_decode(pr, nsa_pools, win_k, win_v, j, page_table, w_c, past)
            shp = (dec_b, dec_t, NSA_KV_HEADS, HEAD_DIM)
            keep = min(NSA_WINDOW, wb + dec_t)
            kw_all = jnp.concatenate([state_nsa_win_k[j], pr["kw"].reshape(shp)], axis=1)
            vw_all = jnp.concatenate([state_nsa_win_v[j], pr["vw"].reshape(shp)], axis=1)
            nsa_new_s.append(tuple(pr[n].reshape(shp) for n in ("kc", "vc", "ks", "vs"))
                             + (kw_all[:, wb + dec_t - keep:], vw_all[:, wb + dec_t - keep:]))
            ys = wo_post_norm(ys, [oc, o_s, o_w, pr["gate"]], w_o, ln_mix_g[i], ln_mix_b[i], alpha)
        yp = ffn_post_norm(yp, ffn_w_gu[i], ffn_w_down[i], ln_ffn_g[i], ln_ffn_b[i], alpha)
        ys = ffn_post_norm(ys, ffn_w_gu[i], ffn_w_down[i], ln_ffn_g[i], ln_ffn_b[i], alpha)

    fp = [jnp.stack(a) for a in zip(*fox_new_p)]
    fs = [jnp.stack(a) for a in zip(*fox_new_s)]
    sp = [jnp.stack(a) for a in zip(*nsa_new_p)]
    ss = [jnp.stack(a) for a in zip(*nsa_new_s)]
    return (yp.reshape(batch, seq, d), ys.reshape(dec_b, dec_t, d), fp[0], fs[0], fp[1], fs[1], fp[2], fs[2],
            sp[0], ss[0], sp[1], ss[1], sp[2], ss[2], sp[3], ss[3], sp[4], ss[4], sp[5], ss[5])
```

```python
import functools

import jax
import jax.numpy as jnp
from jax import lax
from jax.experimental import pallas as pl
from jax.experimental.pallas import tpu as pltpu

F32 = jnp.float32
BF16 = jnp.bfloat16

LANES = 128
HEAD_DIM = 64
NSA_KV_HEADS = 4
NSA_GROUP = 4
NSA_BLOCK = 64
NSA_TOP_N = 16
NSA_WINDOW = 512
PAGE_SIZE = 128
ROPE_THETA = 10000.0
LN_EPS = 1e-5
NEG_BIG = -1e30
SEL_FORCE = 1e9
VMEM_LIMIT_BYTES = 56 * 1024 * 1024

TOKEN_TILE = 512
FOX_TILE = 512
FOX_KEY_TILE = 512
NSA_Q_TILE = 256
NSA_K_TILE = 512
NSA_CMP_TILE = 512
DEC_PAGES_PER_STEP = 16


def _dot(a, b):
    return jnp.dot(a, b, preferred_element_type=F32)


def _dot_nt(a, b):
    return lax.dot_general(a, b, (((1,), (1,)), ((), ())), preferred_element_type=F32)


def _params(*sem):
    return pltpu.CompilerParams(dimension_semantics=sem, vmem_limit_bytes=VMEM_LIMIT_BYTES)


def _layer_norm(x, g, b):
    mu = jnp.mean(x, axis=-1, keepdims=True)
    xc = x - mu
    var = jnp.mean(xc * xc, axis=-1, keepdims=True)
    return xc * lax.rsqrt(var + LN_EPS) * g + b


def _full(shape):
    n = len(shape)
    return pl.BlockSpec(shape, lambda *_: (0,) * n)


def _fox_proj_kernel(x_ref, w_ref, wf_ref, bf_ref, q_ref, k_ref, v_ref, kb_ref, vb_ref, lf_ref):
    aw = q_ref.shape[1]
    x = x_ref[...].astype(BF16)
    q_ref[...] = _dot(x, w_ref[:, 0:aw]).astype(BF16)
    k = _dot(x, w_ref[:, aw:2 * aw])
    k_ref[...] = k
    kb_ref[...] = k.astype(BF16)
    v = _dot(x, w_ref[:, 2 * aw:3 * aw])
    v_ref[...] = v
    vb_ref[...] = v.astype(BF16)
    z = _dot(x, wf_ref[...]) + bf_ref[...]
    lf = jnp.minimum(z, 0.0) - jnp.log1p(jnp.exp(-jnp.abs(z)))
    lf_ref[...] = lf[:, :lf_ref.shape[1]]


def fox_project(x, w_in, b_f):
    m, d = x.shape
    nh = b_f.shape[0]
    aw = (w_in.shape[1] - nh) // 3
    scale = HEAD_DIM ** -0.5
    w_qkv = jnp.concatenate([w_in[:, :aw] * scale, w_in[:, aw:3 * aw]], axis=1).astype(BF16)
    w_f = jnp.pad(w_in[:, 3 * aw:], ((0, 0), (0, LANES - nh))).astype(BF16)
    b = jnp.pad(b_f, (0, LANES - nh)).reshape(1, LANES)
    tm = min(TOKEN_TILE, m)
    row = lambda w: pl.BlockSpec((tm, w), lambda i: (i, 0))
    return pl.pallas_call(
        _fox_proj_kernel,
        grid=(m // tm,),
        in_specs=[row(d), _full(w_qkv.shape), _full(w_f.shape), _full(b.shape)],
        out_specs=[row(aw), row(aw), row(aw), row(aw), row(aw), row(nh)],
        out_shape=[jax.ShapeDtypeStruct((m, aw), BF16), jax.ShapeDtypeStruct((m, aw), F32),
                   jax.ShapeDtypeStruct((m, aw), F32), jax.ShapeDtypeStruct((m, aw), BF16),
                   jax.ShapeDtypeStruct((m, aw), BF16), jax.ShapeDtypeStruct((m, nh), F32)],
        compiler_params=_params("parallel"),
        name="fox_proj",
    )(x, w_qkv, w_f, b)


def _rope(x, cos, sin_signed, lo_half):
    outs = []
    for c in range(x.shape[1] // LANES):
        xc = x[:, c * LANES:(c + 1) * LANES]
        swapped = jnp.where(lo_half, pltpu.roll(xc, LANES - HEAD_DIM // 2, 1),
                            pltpu.roll(xc, HEAD_DIM // 2, 1))
        outs.append(xc * cos + swapped * sin_signed)
    return outs


def _nsa_proj_kernel(x_ref, w_ref, bg_ref, wc_ref, cos_ref, sin_ref,
                     qraw_ref, qrot_ref, kc_ref, vc_ref, ks_ref, vs_ref, kw_ref, vw_ref,
                     ksd_ref, vsd_ref, kwd_ref, vwd_ref, kcsd_ref, vcsd_ref, gate_ref):
    aw = qraw_ref.shape[1]
    kvw = kc_ref.shape[1]
    dw = ksd_ref.shape[1]
    tm = x_ref.shape[0]
    x = x_ref[...].astype(BF16)
    cos = cos_ref[...]
    sin = sin_ref[...]
    lane = lax.broadcasted_iota(jnp.int32, (tm, LANES), 1)
    lo_half = (lane % HEAD_DIM) < (HEAD_DIM // 2)

    def rope_store(val, refs):
        for c, piece in enumerate(_rope(val, cos, sin, lo_half)):
            for r in refs:
                r[:, c * LANES:(c + 1) * LANES] = piece.astype(r.dtype)

    q = _dot(x, w_ref[:, 0:aw])
    qraw_ref[...] = q.astype(BF16)
    rope_store(q, [qrot_ref])
    off = aw
    wc = wc_ref[...]
    nblk = tm // NSA_BLOCK

    def summary(val, out_ref):
        s = jnp.sum(val.reshape(nblk, NSA_BLOCK, kvw) * wc[None], axis=1)
        for g in range(NSA_KV_HEADS):
            sg = s[:, g * HEAD_DIM:(g + 1) * HEAD_DIM].astype(out_ref.dtype)
            out_ref[:, g * LANES:g * LANES + HEAD_DIM] = sg
            out_ref[:, g * LANES + HEAD_DIM:(g + 1) * LANES] = sg

    kc = _dot(x, w_ref[:, off:off + kvw])
    kc_ref[...] = kc
    summary(kc, kcsd_ref)
    vc = _dot(x, w_ref[:, off + kvw:off + 2 * kvw])
    vc_ref[...] = vc
    summary(vc, vcsd_ref)
    rope_store(_dot(x, w_ref[:, off + 2 * kvw:off + 3 * kvw]), [ks_ref])
    vs_ref[...] = _dot(x, w_ref[:, off + 3 * kvw:off + 4 * kvw])
    rope_store(_dot(x, w_ref[:, off + 4 * kvw:off + 5 * kvw]), [kw_ref])
    vw_ref[...] = _dot(x, w_ref[:, off + 5 * kvw:off + 6 * kvw])
    off += 6 * kvw
    rope_store(_dot(x, w_ref[:, off:off + dw]), [ksd_ref])
    vsd_ref[...] = _dot(x, w_ref[:, off + dw:off + 2 * dw]).astype(BF16)
    rope_store(_dot(x, w_ref[:, off + 2 * dw:off + 3 * dw]), [kwd_ref])
    vwd_ref[...] = _dot(x, w_ref[:, off + 3 * dw:off + 4 * dw]).astype(BF16)
    off += 4 * dw
    z = _dot(x, w_ref[:, off:off + LANES]) + bg_ref[...]
    gate_ref[...] = 1.0 / (1.0 + jnp.exp(-z))


def _dup_heads(w):
    d, n = w.shape
    g = n // HEAD_DIM
    w = w.reshape(d, g, 1, HEAD_DIM)
    return jnp.broadcast_to(w, (d, g, 2, HEAD_DIM)).reshape(d, g * LANES)


def nsa_project(x, w_in, b_gate, w_cmp, cos_t, sin_t):
    m, d = x.shape
    ng = b_gate.shape[0]
    aw = (ng // 3) * HEAD_DIM
    kvw = NSA_KV_HEADS * HEAD_DIM
    dw = NSA_KV_HEADS * LANES
    scale = HEAD_DIM ** -0.5
    kv = w_in[:, aw:aw + 6 * kvw]
    cols = [w_in[:, :aw] * scale, kv]
    for i in (2, 3, 4, 5):
        cols.append(_dup_heads(kv[:, i * kvw:(i + 1) * kvw]))
    cols.append(jnp.pad(w_in[:, aw + 6 * kvw:], ((0, 0), (0, LANES - ng))))
    w_all = jnp.concatenate(cols, axis=1).astype(BF16)
    bg = jnp.pad(b_gate, (0, LANES - ng)).reshape(1, LANES)
    wc = jnp.repeat(w_cmp, HEAD_DIM, axis=1)
    tm = min(TOKEN_TILE, m)
    row = lambda w: pl.BlockSpec((tm, w), lambda i: (i, 0))
    sds = jax.ShapeDtypeStruct
    nblk = tm // NSA_BLOCK
    outs = pl.pallas_call(
        _nsa_proj_kernel,
        grid=(m // tm,),
        in_specs=[row(d), _full(w_all.shape), _full(bg.shape), _full(wc.shape), row(LANES), row(LANES)],
        out_specs=[row(aw), row(aw)] + [row(kvw)] * 6 + [row(dw)] * 4
                  + [pl.BlockSpec((nblk, dw), lambda i: (i, 0))] * 2 + [row(LANES)],
        out_shape=[sds((m, aw), BF16), sds((m, aw), BF16)] + [sds((m, kvw), F32)] * 6
                  + [sds((m, dw), BF16)] * 4 + [sds((m // NSA_BLOCK, dw), BF16)] * 2 + [sds((m, LANES), F32)],
        compiler_params=_params("parallel"),
        name="nsa_proj",
    )(x, w_all, bg, wc, cos_t, sin_t)
    names = ("qraw", "qrot", "kc", "vc", "ks", "vs", "kw", "vw", "ksd", "vsd", "kwd", "vwd", "kcsd", "vcsd", "gate")
    return dict(zip(names, outs))


def _wo_ln_kernel(x_ref, *rest, gated, alpha):
    if gated:
        oc, os_, ow, gate_ref, e_ref, w_ref, g_ref, b_ref, y_ref = rest
        gate = gate_ref[...]
        hi = gate.astype(BF16)
        lo = (gate - hi.astype(F32)).astype(BF16)
        ge = _dot(hi, e_ref[...]) + _dot(lo, e_ref[...])
        aw = oc.shape[1]
        o = ge[:, 0:aw] * oc[...] + ge[:, aw:2 * aw] * os_[...] + ge[:, 2 * aw:3 * aw] * ow[...]
    else:
        o_ref, w_ref, g_ref, b_ref, y_ref = rest
        o = o_ref[...]
    h = _dot(o.astype(BF16), w_ref[...])
    y_ref[...] = _layer_norm(alpha * x_ref[...] + h, g_ref[...], b_ref[...])


def wo_post_norm(x, branches, w_o, g, b, alpha):
    m, d = x.shape
    aw = w_o.shape[0]
    tm = min(TOKEN_TILE, m)
    row = lambda w: pl.BlockSpec((tm, w), lambda i: (i, 0))
    gated = len(branches) > 1
    extra, extra_specs = [], []
    if gated:
        n_gate = 3 * aw // HEAD_DIM
        expand = (jnp.arange(LANES)[:, None] == jnp.arange(n_gate * HEAD_DIM)[None, :] // HEAD_DIM).astype(BF16)
        extra, extra_specs = [expand], [_full(expand.shape)]
    return pl.pallas_call(
        functools.partial(_wo_ln_kernel, gated=gated, alpha=alpha),
        grid=(m // tm,),
        in_specs=[row(d)] + [row(a.shape[1]) for a in branches] + extra_specs
                 + [_full(w_o.shape), _full((1, d)), _full((1, d))],
        out_specs=row(d),
        out_shape=jax.ShapeDtypeStruct((m, d), F32),
        compiler_params=_params("parallel"),
        name="wo_post_norm",
    )(x, *branches, *extra, w_o.astype(BF16), g.reshape(1, d), b.reshape(1, d))


def _ffn_kernel(x_ref, wg_ref, wu_ref, wd_ref, g_ref, b_ref, y_ref, acc_ref, *, alpha):
    f = pl.program_id(1)
    x = x_ref[...]
    xb = x.astype(BF16)
    gate = _dot(xb, wg_ref[...])
    up = _dot(xb, wu_ref[...])
    h = gate / (1.0 + jnp.exp(-gate)) * up
    part = _dot(h.astype(BF16), wd_ref[...])

    @pl.when(f == 0)
    def _():
        acc_ref[...] = part

    @pl.when(f > 0)
    def _():
        acc_ref[...] += part

    @pl.when(f == pl.num_programs(1) - 1)
    def _():
        y_ref[...] = _layer_norm(alpha * x + acc_ref[...], g_ref[...], b_ref[...])


def ffn_post_norm(x, w_gu, w_down, g, b, alpha):
    m, d = x.shape
    dff = w_down.shape[0]
    nf = 2
    tf = dff // nf
    tm = min(TOKEN_TILE, m)
    w_gu = w_gu.astype(BF16)
    return pl.pallas_call(
        functools.partial(_ffn_kernel, alpha=alpha),
        grid=(m // tm, nf),
        in_specs=[pl.BlockSpec((tm, d), lambda i, f: (i, 0)),
                  pl.BlockSpec((d, tf), lambda i, f: (0, f)),
                  pl.BlockSpec((d, tf), lambda i, f: (0, nf + f)),
                  pl.BlockSpec((tf, d), lambda i, f: (f, 0)),
                  pl.BlockSpec((1, d), lambda i, f: (0, 0)),
                  pl.BlockSpec((1, d), lambda i, f: (0, 0))],
        out_specs=pl.BlockSpec((tm, d), lambda i, f: (i, 0)),
        out_shape=jax.ShapeDtypeStruct((m, d), F32),
        scratch_shapes=[pltpu.VMEM((tm, d), F32)],
        compiler_params=_params("parallel", "arbitrary"),
        name="ffn_post_norm",
    )(x, w_gu, w_gu, w_down.astype(BF16), g.reshape(1, d), b.reshape(1, d))


def _stack_heads(q_ref, qs_ref, tq):
    lane = lax.broadcasted_iota(jnp.int32, (tq, LANES), 1)
    lo = lane < HEAD_DIM
    for c in range(q_ref.shape[1] // LANES):
        qc = q_ref[:, c * LANES:(c + 1) * LANES].astype(F32)
        qs_ref[(2 * c) * tq:(2 * c + 1) * tq, 0:LANES] = jnp.where(lo, qc, 0.0).astype(qs_ref.dtype)
        qs_ref[(2 * c + 1) * tq:(2 * c + 2) * tq, 0:LANES] = jnp.where(lo, 0.0, qc).astype(qs_ref.dtype)


def _unstack_heads(vals, o_ref):
    tq = vals.shape[1]
    lane = lax.broadcasted_iota(jnp.int32, (tq, LANES), 1)
    lo = lane < HEAD_DIM
    for c in range(vals.shape[0] // 2):
        o_ref[:, c * LANES:(c + 1) * LANES] = jnp.where(lo, vals[2 * c], vals[2 * c + 1])


def _softmax_init(m_ref, l_ref, acc_ref):
    m_ref[...] = jnp.full(m_ref.shape, NEG_BIG, F32)
    l_ref[...] = jnp.zeros(l_ref.shape, F32)
    acc_ref[...] = jnp.zeros(acc_ref.shape, F32)


def _softmax_update(s, v_tile, m_ref, l_ref, acc_ref):
    tk = s.shape[1]
    m_prev = m_ref[...]
    m_new = jnp.maximum(m_prev, jnp.max(s, axis=-1, keepdims=True))
    alpha = jnp.exp(m_prev - m_new)
    p = jnp.exp((s - jnp.tile(m_new, (1, tk // LANES))).astype(BF16))
    p_sum = p[:, 0:LANES]
    for c in range(1, tk // LANES):
        p_sum = p_sum + p[:, c * LANES:(c + 1) * LANES]
    l_ref[...] = alpha * l_ref[...] + jnp.sum(p_sum.astype(F32), axis=-1, keepdims=True)
    acc_ref[...] = alpha * acc_ref[...] + _dot(p, v_tile)
    m_ref[...] = m_new


def _attn_scratch(rows):
    return [pltpu.VMEM((rows, LANES), BF16), pltpu.VMEM((rows, LANES), F32),
            pltpu.VMEM((rows, LANES), F32), pltpu.VMEM((rows, LANES), F32)]


def _fox_attn_kernel(q_ref, k_ref, v_ref, f_ref, o_ref, qs_ref, m_ref, l_ref, acc_ref, *, t, tk):
    qi = pl.program_id(2)
    _stack_heads(q_ref, qs_ref, t)
    _softmax_init(m_ref, l_ref, acc_ref)
    q0 = pl.multiple_of(qi * t, t)
    f_q0 = f_ref[0, 0, :, pl.ds(q0, LANES)][:, 0:1]
    kt_diag = q0 // tk

    def tile(kt, causal):
        k0 = pl.multiple_of(kt * tk, tk)
        s = _dot_nt(qs_ref[...], k_ref[pl.ds(k0, tk), :])
        bias = f_q0 - f_ref[0, 0, :, pl.ds(k0, tk)]
        s3 = s.reshape(2, t, tk) + bias[:, None, :]
        if causal:
            qpos = q0 + lax.broadcasted_iota(jnp.int32, (t, tk), 0)
            kpos = k0 + lax.broadcasted_iota(jnp.int32, (t, tk), 1)
            s3 = jnp.where((kpos <= qpos)[None], s3, NEG_BIG)
        _softmax_update(s3.reshape(2 * t, tk), v_ref[pl.ds(k0, tk), :], m_ref, l_ref, acc_ref)

    def body(kt, carry):
        tile(kt, False)
        return carry

    lax.fori_loop(0, kt_diag, body, 0)
    tile(kt_diag, True)
    _unstack_heads((acc_ref[...] / l_ref[...]).reshape(2, t, LANES), o_ref)


def fox_prompt_attention(q, k, v, f_cum, batch):
    m, aw = q.shape
    seq = m // batch
    t = FOX_TILE
    nq = seq // t
    npair = aw // LANES
    f = jnp.swapaxes(f_cum, 1, 2).reshape(batch, npair, 2, seq)
    return pl.pallas_call(
        functools.partial(_fox_attn_kernel, t=t, tk=FOX_KEY_TILE),
        grid=(batch, npair, nq),
        in_specs=[pl.BlockSpec((t, LANES), lambda b, c, i: (b * nq + i, c)),
                  pl.BlockSpec((seq, LANES), lambda b, c, i: (b, c)),
                  pl.BlockSpec((seq, LANES), lambda b, c, i: (b, c)),
                  pl.BlockSpec((1, 1, 2, seq), lambda b, c, i: (b, c, 0, 0))],
        out_specs=pl.BlockSpec((t, LANES), lambda b, c, i: (b * nq + i, c)),
        out_shape=jax.ShapeDtypeStruct((m, aw), F32),
        scratch_shapes=_attn_scratch(2 * t),
        compiler_params=_params("parallel", "parallel", "arbitrary"),
        name="fox_prompt_attn",
    )(q, k, v, f)


def _top_n_mask(score, blk, n_sel):
    sel = jnp.zeros(score.shape, F32)
    for _ in range(n_sel):
        hit = blk == jnp.argmax(score, axis=-1, keepdims=True)
        sel = jnp.where(hit, 1.0, sel)
        score = jnp.where(hit, -jnp.inf, score)
    return sel


def _block_scores(pn_sum, blk, qpos):
    cur = qpos // NSA_BLOCK
    forced = (blk == 0) | (blk == cur) | (blk == cur - 1)
    return jnp.where(blk <= cur, jnp.where(forced, SEL_FORCE, pn_sum), -SEL_FORCE)


def _masked_softmax(s, vis):
    s = jnp.where(vis, s, NEG_BIG)
    p = jnp.where(vis, jnp.exp(s - jnp.max(s, axis=-1, keepdims=True)), 0.0)
    return p / jnp.maximum(jnp.sum(p, axis=-1, keepdims=True), 1e-30)


def _nsa_cmp_kernel(q_ref, kk_ref, vv_ref, oc_ref, sel_ref, qs_ref, *, tq):
    qi = pl.program_id(2)
    nb = kk_ref.shape[0]
    _stack_heads(q_ref, qs_ref, tq)
    s3 = _dot_nt(qs_ref[...], kk_ref[...]).reshape(NSA_GROUP, tq, nb)
    blk = lax.broadcasted_iota(jnp.int32, (tq, nb), 1)
    qpos = qi * tq + lax.broadcasted_iota(jnp.int32, (tq, nb), 0)
    vis = (blk + 1) * NSA_BLOCK - 1 <= qpos
    pn = _masked_softmax(s3, vis[None])
    oc = _dot(pn.reshape(NSA_GROUP * tq, nb).astype(BF16), vv_ref[...])
    _unstack_heads(oc.reshape(NSA_GROUP, tq, LANES), oc_ref)
    score = _block_scores(jnp.sum(pn, axis=0), blk, qpos)
    sel_ref[0, 0] = _top_n_mask(score, blk, min(NSA_TOP_N, nb)).astype(sel_ref.dtype)


def nsa_prompt_compress_select(qraw, kcsd, vcsd, batch):
    m, aw = qraw.shape
    seq = m // batch
    nb = seq // NSA_BLOCK
    assert nb <= LANES, "block axis is mapped onto one vreg lane row"
    tq = NSA_CMP_TILE
    nq = seq // tq
    gw = NSA_GROUP * HEAD_DIM
    ng = aw // gw
    return pl.pallas_call(
        functools.partial(_nsa_cmp_kernel, tq=tq),
        grid=(batch, ng, nq),
        in_specs=[pl.BlockSpec((tq, gw), lambda b, g, i: (b * nq + i, g)),
                  pl.BlockSpec((nb, LANES), lambda b, g, i: (b, g)),
                  pl.BlockSpec((nb, LANES), lambda b, g, i: (b, g))],
        out_specs=[pl.BlockSpec((tq, gw), lambda b, g, i: (b * nq + i, g)),
                   pl.BlockSpec((1, 1, tq, nb), lambda b, g, i: (b, g, i, 0))],
        out_shape=[jax.ShapeDtypeStruct((m, aw), F32), jax.ShapeDtypeStruct((batch, ng, seq, nb), BF16)],
        scratch_shapes=[pltpu.VMEM((NSA_GROUP * tq, LANES), BF16)],
        compiler_params=_params("parallel", "parallel", "parallel"),
        name="nsa_prompt_cmp_select",
    )(qraw, kcsd, vcsd)


def _nsa_sel_kernel(q_ref, k_ref, v_ref, sel_ref, o_ref, qs_ref, m_ref, l_ref, acc_ref, *, tq, tk):
    qi = pl.program_id(2)
    _stack_heads(q_ref, qs_ref, tq)
    _softmax_init(m_ref, l_ref, acc_ref)
    nb = sel_ref.shape[3]
    sel_bias = ((sel_ref[0, 0].astype(F32) - 1.0) * (-NEG_BIG)).astype(qs_ref.dtype)
    for i in range(NSA_GROUP):
        qs_ref[i * tq:(i + 1) * tq, LANES:LANES + nb] = sel_bias
    q0 = qi * tq
    kt_diag = q0 // tk

    def tile(kt, causal):
        k0 = pl.multiple_of(kt * tk, tk)
        s = _dot_nt(qs_ref[...], k_ref[pl.ds(k0, tk), :])
        if causal:
            qpos = q0 + lax.broadcasted_iota(jnp.int32, (tq, tk), 0)
            kpos = k0 + lax.broadcasted_iota(jnp.int32, (tq, tk), 1)
            s = jnp.where((kpos <= qpos)[None], s.reshape(NSA_GROUP, tq, tk), NEG_BIG).reshape(NSA_GROUP * tq, tk)
        _softmax_update(s, v_ref[pl.ds(k0, tk), :], m_ref, l_ref, acc_ref)

    def body(kt, carry):
        tile(kt, False)
        return carry

    lax.fori_loop(0, kt_diag, body, 0)
    tile(kt_diag, True)
    _unstack_heads((acc_ref[...] / l_ref[...]).reshape(NSA_GROUP, tq, LANES), o_ref)


def nsa_prompt_selected(qrot, ksd, vsd, sel, batch):
    m, aw = qrot.shape
    seq = m // batch
    nb = seq // NSA_BLOCK
    tq, tk = NSA_Q_TILE, NSA_K_TILE
    nq = seq // tq
    gw = NSA_GROUP * HEAD_DIM
    ng = aw // gw
    onehot = (jnp.arange(seq)[:, None] // NSA_BLOCK == jnp.arange(nb)[None, :]).astype(BF16)
    k_ext = jnp.concatenate([ksd.reshape(batch, seq, ng, LANES),
                             jnp.broadcast_to(onehot[None, :, None, :], (batch, seq, ng, nb))], axis=-1)
    kw = LANES + nb
    scratch = _attn_scratch(NSA_GROUP * tq)
    scratch[0] = pltpu.VMEM((NSA_GROUP * tq, kw), BF16)
    return pl.pallas_call(
        functools.partial(_nsa_sel_kernel, tq=tq, tk=tk),
        grid=(batch, ng, nq),
        in_specs=[pl.BlockSpec((tq, gw), lambda b, g, i: (b * nq + i, g)),
                  pl.BlockSpec((seq, kw), lambda b, g, i: (b, g)),
                  pl.BlockSpec((seq, LANES), lambda b, g, i: (b, g)),
                  pl.BlockSpec((1, 1, tq, nb), lambda b, g, i: (b, g, i, 0))],
        out_specs=pl.BlockSpec((tq, gw), lambda b, g, i: (b * nq + i, g)),
        out_shape=jax.ShapeDtypeStruct((m, aw), F32),
        scratch_shapes=scratch,
        compiler_params=_params("parallel", "parallel", "arbitrary"),
        name="nsa_prompt_selected",
    )(qrot, k_ext.reshape(m, ng * kw), vsd, sel)


def _nsa_win_kernel(q_ref, k_ref, v_ref, o_ref, qs_ref, *, tq):
    qi = pl.program_id(2)
    span = NSA_WINDOW + tq
    _stack_heads(q_ref, qs_ref, tq)
    k0 = pl.multiple_of(jnp.maximum(qi * tq - NSA_WINDOW, 0), tq)
    s = _dot_nt(qs_ref[...], k_ref[pl.ds(k0, span), :])
    dist = (qi * tq - k0 + lax.broadcasted_iota(jnp.int32, (tq, span), 0)
            - lax.broadcasted_iota(jnp.int32, (tq, span), 1))
    vis = (dist >= 0) & (dist < NSA_WINDOW)
    s = jnp.where(vis[None], s.reshape(NSA_GROUP, tq, span), NEG_BIG).reshape(NSA_GROUP * tq, span)
    p = jnp.exp((s - jnp.max(s, axis=-1, keepdims=True)).astype(BF16))
    p_sum = p[:, 0:LANES]
    for c in range(1, span // LANES):
        p_sum = p_sum + p[:, c * LANES:(c + 1) * LANES]
    o = _dot(p, v_ref[pl.ds(k0, span), :]) / jnp.sum(p_sum.astype(F32), axis=-1, keepdims=True)
    _unstack_heads(o.reshape(NSA_GROUP, tq, LANES), o_ref)


def nsa_prompt_window(qrot, kwd, vwd, batch):
    m, aw = qrot.shape
    seq = m // batch
    tq = NSA_Q_TILE
    assert NSA_WINDOW % tq == 0 and seq >= NSA_WINDOW + tq
    nq = seq // tq
    gw = NSA_GROUP * HEAD_DIM
    ng = aw // gw
    return pl.pallas_call(
        functools.partial(_nsa_win_kernel, tq=tq),
        grid=(batch, ng, nq),
        in_specs=[pl.BlockSpec((tq, gw), lambda b, g, i: (b * nq + i, g)),
                  pl.BlockSpec((seq, LANES), lambda b, g, i: (b, g)),
                  pl.BlockSpec((seq, LANES), lambda b, g, i: (b, g))],
        out_specs=pl.BlockSpec((tq, gw), lambda b, g, i: (b * nq + i, g)),
        out_shape=jax.ShapeDtypeStruct((m, aw), F32),
        scratch_shapes=[pltpu.VMEM((NSA_GROUP * tq, LANES), BF16)],
        compiler_params=_params("parallel", "parallel", "parallel"),
        name="nsa_prompt_window",
    )(qrot, kwd, vwd)


def _tile_rows(x, n):
    return jnp.concatenate([x] * n, axis=0)


def _column_update(state, cols, v_rows):
    m_prev, l_prev, acc = state
    m_new = m_prev
    for c in cols:
        m_new = jnp.maximum(m_new, c)
    alpha = jnp.exp(m_prev - m_new)
    l_new = alpha * l_prev
    acc = alpha * acc
    for j, c in enumerate(cols):
        p = jnp.exp(c - m_new)
        l_new = l_new + p
        acc = acc + p * v_rows[j:j + 1, :]
    return m_new, l_new, acc


def _block_update(state, s, v_t):
    m_prev, l_prev, acc = state
    m_new = jnp.maximum(m_prev, jnp.max(s, axis=-1, keepdims=True))
    alpha = jnp.exp(m_prev - m_new)
    p = jnp.exp(s - m_new)
    l_new = alpha * l_prev + jnp.sum(p, axis=-1, keepdims=True)
    acc = alpha * acc + _dot_nt(p.astype(BF16), v_t)
    return m_new, l_new, acc


def _fox_dec_kernel(pt_ref, qbd_ref, kn_ref, vn_ref, fp_ref, fn_ref, *rest, pp, n_new, heads):
    k_refs, v_refs = rest[:pp], rest[pp:2 * pp]
    o_ref, m_ref, l_ref, acc_ref = rest[2 * pp:]
    step = pl.program_id(1)
    rows = n_new * heads
    qbd = qbd_ref[0]
    fn = fn_ref[0]
    f_ref0 = fn[:, 0:1]

    @pl.when(step == 0)
    def _():
        qf = qbd.astype(F32)
        kn = kn_ref[0]
        rowq = lax.broadcasted_iota(jnp.int32, (rows, 1), 0) // heads
        cols = []
        for j in range(n_new):
            sj = jnp.sum(qf * kn[j:j + 1, :], axis=-1, keepdims=True)
            sj = sj + _tile_rows(f_ref0 - fn[:, j:j + 1], n_new)
            cols.append(jnp.where(j <= rowq, sj, NEG_BIG))
        init = (jnp.full((rows, 1), NEG_BIG, F32), jnp.zeros((rows, 1), F32),
                jnp.zeros(acc_ref.shape, F32))
        m, l, acc = _column_update(init, cols, vn_ref[0])
        m_ref[...] = m
        l_ref[...] = l
        acc_ref[...] = acc

    state = (m_ref[...], l_ref[...], acc_ref[...])
    for i in range(pp):
        s = _dot(qbd, k_refs[i][0, 0].astype(BF16))
        bias = f_ref0 - fp_ref[0][:, i * PAGE_SIZE:(i + 1) * PAGE_SIZE]
        state = _block_update(state, s + _tile_rows(bias, n_new), v_refs[i][0, 0].astype(BF16))
    m_ref[...], l_ref[...], acc_ref[...] = state

    @pl.when(step == pl.num_programs(1) - 1)
    def _():
        w = acc_ref.shape[1]
        o = acc_ref[...] / l_ref[...]
        own = (lax.broadcasted_iota(jnp.int32, (rows, w), 0) % heads
               == lax.broadcasted_iota(jnp.int32, (rows, w), 1) // HEAD_DIM)
        o_ref[0] = jnp.sum(jnp.where(own, o, 0.0).reshape(n_new, heads, w), axis=1)


def _block_diag_queries(q, n_lane_groups, order):
    bsz, tq, h, dh = q.shape
    r = h // n_lane_groups
    onehot = (jnp.arange(h)[:, None] // r == jnp.arange(n_lane_groups)[None, :]).astype(q.dtype)
    x = q[:, :, :, None, :] * onehot[None, None, :, :, None]
    if order == "rqg":
        x = x.reshape(bsz, tq, n_lane_groups, r, n_lane_groups, dh).transpose(0, 3, 1, 2, 4, 5)
    return x.reshape(bsz, tq * h, n_lane_groups * dh)


def fox_decode_attention(q, k_new, v_new, f_past, f_new, cache_k, cache_v, layer, page_table):
    bsz, n_new, aw = q.shape
    heads = aw // HEAD_DIM
    n_pages = page_table.shape[1]
    pp = min(DEC_PAGES_PER_STEP, n_pages)
    rows = n_new * heads
    qbd = _block_diag_queries(q.reshape(bsz, n_new, heads, HEAD_DIM), heads, "qh")
    fn = jnp.pad(f_new, ((0, 0), (0, 0), (0, LANES - n_new)))
    page = lambda i: pl.BlockSpec(
        (1, 1, aw, PAGE_SIZE), lambda b, s, pt: (layer, pt[b * n_pages + s * pp + i], 0, 0))
    per_b = lambda shape: pl.BlockSpec((1,) + shape, lambda b, s, pt: (b, 0, 0))
    grid_spec = pltpu.PrefetchScalarGridSpec(
        num_scalar_prefetch=1,
        grid=(bsz, n_pages // pp),
        in_specs=[per_b((rows, aw)), per_b((n_new, aw)), per_b((n_new, aw)),
                  pl.BlockSpec((1, heads, pp * PAGE_SIZE), lambda b, s, pt: (b, 0, s)),
                  per_b((heads, LANES))]
                 + [page(i) for i in range(pp)] * 2,
        out_specs=per_b((n_new, aw)),
        scratch_shapes=[pltpu.VMEM((rows, 1), F32), pltpu.VMEM((rows, 1), F32), pltpu.VMEM((rows, aw), F32)],
    )
    return pl.pallas_call(
        functools.partial(_fox_dec_kernel, pp=pp, n_new=n_new, heads=heads),
        grid_spec=grid_spec,
        out_shape=jax.ShapeDtypeStruct((bsz, n_new, aw), F32),
        compiler_params=_params("parallel", "arbitrary"),
        name="fox_decode_attn",
    )(page_table.reshape(-1), qbd, k_new, v_new, f_past, fn, *([cache_k] * pp), *([cache_v] * pp))


def _fold_own_group(acc, n_groups):
    rows, w = acc.shape
    own = (lax.broadcasted_iota(jnp.int32, (rows, w), 0) % n_groups
           == lax.broadcasted_iota(jnp.int32, (rows, w), 1) // HEAD_DIM)
    a = jnp.where(own, acc, 0.0)
    out = a[:, 0:HEAD_DIM]
    for g in range(1, n_groups):
        out = out + a[:, g * HEAD_DIM:(g + 1) * HEAD_DIM]
    return out


def _nsa_dec_cmp_kernel(pt_ref, qbd_ref, kn_ref, vn_ref, wt_ref, e_ref, *rest, n_pages, n_new, past):
    k_refs, v_refs = rest[:n_pages], rest[n_pages:2 * n_pages]
    oc_ref, sel_ref = rest[2 * n_pages:]
    g = NSA_KV_HEADS
    rows = qbd_ref.shape[1]
    wt = wt_ref[...]
    blk_new = past // NSA_BLOCK

    def summaries(refs, new_ref):
        acc = jnp.zeros((wt.shape[0], LANES), F32)
        for p in range(n_pages):
            x = refs[p][0, 0] * wt
            hi = x.astype(BF16)
            lo = (x - hi.astype(F32)).astype(BF16)
            e_p = e_ref[:, p * PAGE_SIZE:(p + 1) * PAGE_SIZE]
            acc = acc + _dot_nt(hi, e_p) + _dot_nt(lo, e_p)
        new_t = new_ref[0]
        col = new_t[:, 0:1] * wt[:, 0:1]
        for j in range(1, n_new):
            col = col + new_t[:, j:j + 1] * wt[:, j:j + 1]
        lane = lax.broadcasted_iota(jnp.int32, acc.shape, 1)
        return jnp.where(lane == blk_new, col, acc)

    kcs_t = summaries(k_refs, kn_ref)
    vcs_t = summaries(v_refs, vn_ref)
    s = _dot(qbd_ref[0], kcs_t.astype(BF16))
    blk = lax.broadcasted_iota(jnp.int32, (rows, LANES), 1)
    qpos = past + (lax.broadcasted_iota(jnp.int32, (rows, LANES), 0) % (n_new * g)) // g
    pn = _masked_softmax(s, (blk + 1) * NSA_BLOCK - 1 <= qpos)
    oc_ref[0] = _fold_own_group(_dot_nt(pn.astype(BF16), vcs_t.astype(BF16)), g)
    qg = n_new * g
    pn_sum = pn[0:qg]
    for r in range(1, NSA_GROUP):
        pn_sum = pn_sum + pn[r * qg:(r + 1) * qg]
    blk_q = lax.broadcasted_iota(jnp.int32, (qg, LANES), 1)
    qpos_q = past + lax.broadcasted_iota(jnp.int32, (qg, LANES), 0) // g
    score = _block_scores(pn_sum, blk_q, qpos_q)
    sel = _top_n_mask(score, blk_q, NSA_TOP_N)
    sel_ref[0] = _tile_rows(sel, NSA_GROUP).astype(sel_ref.dtype)


def _nsa_dec_attn_kernel(pt_ref, qbd_ref, sel_ref, e_ref, ksn_ref, vsn_ref, kwn_ref, vwn_ref,
                         wk_ref, wv_ref, *rest, n_pages, n_new, past):
    k_refs, v_refs = rest[:n_pages], rest[n_pages:2 * n_pages]
    os_ref, ow_ref = rest[2 * n_pages:]
    g = NSA_KV_HEADS
    qbd = qbd_ref[0]
    rows, w = qbd.shape
    qf = qbd.astype(F32)
    sel = sel_ref[0]
    rowq = (lax.broadcasted_iota(jnp.int32, (rows, 1), 0) % (n_new * g)) // g
    empty = (jnp.full((rows, 1), NEG_BIG, F32), jnp.zeros((rows, 1), F32), jnp.zeros((rows, w), F32))

    state = empty
    for p in range(n_pages):
        s = _dot(qbd, k_refs[p][0, 0].astype(BF16))
        chosen = _dot(sel, e_ref[:, p * PAGE_SIZE:(p + 1) * PAGE_SIZE])
        state = _block_update(state, s + (chosen - 1.0) * (-NEG_BIG), v_refs[p][0, 0].astype(BF16))
    blk_new = past // NSA_BLOCK
    new_sel = sel[:, blk_new:blk_new + 1].astype(F32) > 0.5
    ksn = ksn_ref[0]
    cols = []
    for j in range(n_new):
        sj = jnp.sum(qf * ksn[j:j + 1, :], axis=-1, keepdims=True)
        cols.append(jnp.where((j <= rowq) & new_sel, sj, NEG_BIG))
    m, l, acc = _column_update(state, cols, vsn_ref[0])
    os_ref[0] = _fold_own_group(acc / l, g)

    kwn = kwn_ref[0]
    cols = []
    for j in range(n_new):
        sj = jnp.sum(qf * kwn[j:j + 1, :], axis=-1, keepdims=True)
        cols.append(jnp.where(j <= rowq, sj, NEG_BIG))
    state = _column_update(empty, cols, vwn_ref[0])
    wb = wk_ref.shape[3]
    s = _dot(qbd, wk_ref[0, 0].astype(BF16))
    slot = lax.broadcasted_iota(jnp.int32, (rows, wb), 1)
    dist = wb + rowq - slot
    vis = (dist >= 0) & (dist < NSA_WINDOW) & (past - wb + slot >= 0)
    m, l, acc = _block_update(state, jnp.where(vis, s, NEG_BIG), wv_ref[0, 0].astype(BF16))
    ow_ref[0] = _fold_own_group(acc / l, g)


def nsa_decode(pr, caches, win_k, win_v, layer, page_table, w_cmp, past):
    cmp_k, cmp_v, slc_k, slc_v = caches
    bsz, n_pages = page_table.shape
    kvw = NSA_KV_HEADS * HEAD_DIM
    n_new = pr["kc"].shape[0] // bsz
    heads = pr["qraw"].shape[1] // HEAD_DIM
    rows = n_new * heads
    assert past % NSA_BLOCK == 0 and n_new <= NSA_BLOCK and past // NSA_BLOCK < LANES
    new = lambda name: pr[name].reshape(bsz, n_new, kvw)
    new_t = lambda name: jnp.swapaxes(new(name), 1, 2)
    qbd = lambda name: _block_diag_queries(
        pr[name].reshape(bsz, n_new, heads, HEAD_DIM), NSA_KV_HEADS, "rqg")
    wt = jnp.tile(jnp.repeat(w_cmp, HEAD_DIM, axis=1).T, (1, PAGE_SIZE // NSA_BLOCK))
    page = lambda i: pl.BlockSpec((1, 1, kvw, PAGE_SIZE), lambda b, pt: (layer, pt[b * n_pages + i], 0, 0))
    per_b = lambda shape: pl.BlockSpec((1,) + shape, lambda b, pt: (b, 0, 0))
    const = lambda shape: pl.BlockSpec(shape, lambda b, pt: (0,) * len(shape))
    pt_flat = page_table.reshape(-1)
    sds = jax.ShapeDtypeStruct
    n_tok = n_pages * PAGE_SIZE
    expand = (jnp.arange(LANES)[:, None] == jnp.arange(n_tok)[None, :] // NSA_BLOCK).astype(BF16)

    oc, sel = pl.pallas_call(
        functools.partial(_nsa_dec_cmp_kernel, n_pages=n_pages, n_new=n_new, past=past),
        grid_spec=pltpu.PrefetchScalarGridSpec(
            num_scalar_prefetch=1, grid=(bsz,),
            in_specs=[per_b((rows, kvw)), per_b((kvw, n_new)), per_b((kvw, n_new)), const(wt.shape),
                      const(expand.shape)]
                     + [page(i) for i in range(n_pages)] * 2,
            out_specs=[per_b((rows, HEAD_DIM)), per_b((rows, LANES))]),
        out_shape=[sds((bsz, rows, HEAD_DIM), F32), sds((bsz, rows, LANES), BF16)],
        compiler_params=_params("parallel"),
        name="nsa_decode_cmp_select",
    )(pt_flat, qbd("qraw"), new_t("kc"), new_t("vc"), wt, expand, *([cmp_k] * n_pages), *([cmp_v] * n_pages))

    wb = win_k.shape[3]
    win = pl.BlockSpec((1, 1, kvw, wb), lambda b, pt: (layer, b, 0, 0))
    o_s, o_w = pl.pallas_call(
        functools.partial(_nsa_dec_attn_kernel, n_pages=n_pages, n_new=n_new, past=past),
        grid_spec=pltpu.PrefetchScalarGridSpec(
            num_scalar_prefetch=1, grid=(bsz,),
            in_specs=[per_b((rows, kvw)), per_b((rows, LANES)), const(expand.shape)]
                     + [per_b((n_new, kvw))] * 4 + [win, win]
                     + [page(i) for i in range(n_pages)] * 2,
            out_specs=[per_b((rows, HEAD_DIM))] * 2),
        out_shape=[sds((bsz, rows, HEAD_DIM), F32)] * 2,
        compiler_params=_params("parallel"),
        name="nsa_decode_attn",
    )(pt_flat, qbd("qrot"), sel, expand, new("ks"), new("vs"), new("kw"), new("vw"),
      win_k, win_v, *([slc_k] * n_pages), *([slc_v] * n_pages))

    def to_tokens(o):
        o = o.reshape(bsz, NSA_GROUP, n_new, NSA_KV_HEADS, HEAD_DIM)
        return o.transpose(0, 2, 3, 1, 4).reshape(bsz * n_new, heads * HEAD_DIM)

    return to_tokens(oc), to_tokens(o_s), to_tokens(o_w)


def _rope_tables(pos):
    half = HEAD_DIM // 2
    inv_freq = ROPE_THETA ** (-jnp.arange(half, dtype=F32) / half)
    ang = pos.astype(F32)[:, None] * inv_freq[None, :]
    cos, sin = jnp.cos(ang), jnp.sin(ang)
    reps = LANES // HEAD_DIM
    return (jnp.tile(cos, (1, 2 * reps)), jnp.tile(jnp.concatenate([-sin, sin], axis=1), (1, reps)))


def kernel(x_prompt, x_sample, cache_fox_k, cache_fox_v, cache_fox_logf, cache_nsa_cmp_k, cache_nsa_cmp_v, cache_nsa_slc_k, cache_nsa_slc_v, state_nsa_win_k, state_nsa_win_v, page_table, fox_w_in, fox_b_f, fox_w_o, nsa_w_in, nsa_b_gate, nsa_w_cmp, nsa_w_o, ffn_w_gu, ffn_w_down, ln_mix_g, ln_mix_b, ln_ffn_g, ln_ffn_b):
    batch, seq, d = x_prompt.shape
    dec_b, dec_t, _ = x_sample.shape
    depth = ffn_w_gu.shape[0]
    heads = fox_b_f.shape[1]
    n_pages = page_table.shape[1]
    past = n_pages * PAGE_SIZE
    alpha = (2 * depth) ** 0.25
    n_phys = cache_fox_k.shape[1]
    kvw = NSA_KV_HEADS * HEAD_DIM

    yp = x_prompt.reshape(batch * seq, d)
    ys = x_sample.reshape(dec_b * dec_t, d)
    key_minor = lambda c: jnp.transpose(c, (0, 1, 3, 4, 2)).reshape(c.shape[0], c.shape[1], -1, c.shape[2])
    fox_k_pool, fox_v_pool = key_minor(cache_fox_k), key_minor(cache_fox_v)
    nsa_pools = [key_minor(c) for c in (cache_nsa_cmp_k, cache_nsa_cmp_v, cache_nsa_slc_k, cache_nsa_slc_v)]
    wb = state_nsa_win_k.shape[2]
    win_k, win_v = key_minor(state_nsa_win_k), key_minor(state_nsa_win_v)
    cos_p, sin_p = _rope_tables(jnp.tile(jnp.arange(seq), batch))
    cos_s, sin_s = _rope_tables(jnp.tile(past + jnp.arange(dec_t), dec_b))

    fox_new_p, fox_new_s, nsa_new_p, nsa_new_s = [], [], [], []
    for i in range(depth):
        j = i // 2
        if i % 2 == 0:
            w_in, b_f, w_o = fox_w_in[j], fox_b_f[j], fox_w_o[j]
            q, k, v, kb, vb, lf = fox_project(yp, w_in, b_f)
            lf3 = lf.reshape(batch, seq, heads)
            o = fox_prompt_attention(q, kb, vb, jnp.cumsum(lf3, axis=1), batch)
            fox_new_p.append((k.reshape(batch, seq, heads, HEAD_DIM), v.reshape(batch, seq, heads, HEAD_DIM), lf3))
            yp = wo_post_norm(yp, [o], w_o, ln_mix_g[i], ln_mix_b[i], alpha)

            q, k, v, _, _, lf = fox_project(ys, w_in, b_f)
            lf3 = lf.reshape(dec_b, dec_t, heads)
            lf_past = cache_fox_logf[j][page_table].reshape(dec_b, past, heads).astype(F32)
            f_cum = jnp.swapaxes(jnp.cumsum(jnp.concatenate([lf_past, lf3], axis=1), axis=1), 1, 2)
            o = fox_decode_attention(q.reshape(dec_b, dec_t, -1), k.reshape(dec_b, dec_t, -1),
                                     v.reshape(dec_b, dec_t, -1), f_cum[:, :, :past], f_cum[:, :, past:],
                                     fox_k_pool, fox_v_pool, j, page_table)
            fox_new_s.append((k.reshape(dec_b, dec_t, heads, HEAD_DIM), v.reshape(dec_b, dec_t, heads, HEAD_DIM), lf3))
            ys = wo_post_norm(ys, [o.reshape(dec_b * dec_t, -1)], w_o, ln_mix_g[i], ln_mix_b[i], alpha)
        else:
            w_in, b_g, w_c, w_o = nsa_w_in[j], nsa_b_gate[j], nsa_w_cmp[j], nsa_w_o[j]
            pr = nsa_project(yp, w_in, b_g, w_c, cos_p, sin_p)
            oc, sel = nsa_prompt_compress_select(pr["qraw"], pr["kcsd"], pr["vcsd"], batch)
            o_s = nsa_prompt_selected(pr["qrot"], pr["ksd"], pr["vsd"], sel, batch)
            o_w = nsa_prompt_window(pr["qrot"], pr["kwd"], pr["vwd"], batch)
            shp = (batch, seq, NSA_KV_HEADS, HEAD_DIM)
            keep = min(NSA_WINDOW, seq)
            nsa_new_p.append(tuple(pr[n].reshape(shp) for n in ("kc", "vc", "ks", "vs"))
                             + tuple(pr[n].reshape(shp)[:, seq - keep:] for n in ("kw", "vw")))
            yp = wo_post_norm(yp, [oc, o_s, o_w, pr["gate"]], w_o, ln_mix_g[i], ln_mix_b[i], alpha)

            pr = nsa_project(ys, w_in, b_g, w_c, cos_s, sin_s)
            oc, o_s, o_w = nsa_decode(pr, nsa_pools, win_k, win_v, j, page_table, w_c, past)
            shp = (dec_b, dec_t, NSA_KV_HEADS, HEAD_DIM)
            nsa_new_s.append(tuple(pr[n].reshape(shp) for n in ("kc", "vc", "ks", "vs", "kw", "vw")))
            ys = wo_post_norm(ys, [oc, o_s, o_w, pr["gate"]], w_o, ln_mix_g[i], ln_mix_b[i], alpha)
        yp = ffn_post_norm(yp, ffn_w_gu[i], ffn_w_down[i], ln_ffn_g[i], ln_ffn_b[i], alpha)
        ys = ffn_post_norm(ys, ffn_w_gu[i], ffn_w_down[i], ln_ffn_g[i], ln_ffn_b[i], alpha)

    fp = [jnp.stack(a) for a in zip(*fox_new_p)]
    fs = [jnp.stack(a) for a in zip(*fox_new_s)]
    sp = [jnp.stack(a) for a in zip(*nsa_new_p)]
    ss = [jnp.stack(a) for a in zip(*nsa_new_s)]
    keep = min(NSA_WINDOW, wb + dec_t)
    for idx, state in ((4, state_nsa_win_k), (5, state_nsa_win_v)):
        ss[idx] = jnp.concatenate([state, ss[idx]], axis=2)[:, :, wb + dec_t - keep:]
    return (yp.reshape(batch, seq, d), ys.reshape(dec_b, dec_t, d), fp[0], fs[0], fp[1], fs[1], fp[2], fs[2],
            sp[0], ss[0], sp[1], ss[1], sp[2], ss[2], sp[3], ss[3], sp[4], ss[4], sp[5], ss[5])
```
